```python
import jax, jax.numpy as jnp
from jax import lax
import numpy as np

D_MODEL = 1024
BATCH = 8
SEQ = 8192
DEPTH = 4

MEM_LEN = 256
MIX_WIDTH = 768
N_BRANCHES = 3
RNN_BLOCKS = 8
RNN_BLOCK = MIX_WIDTH // RNN_BLOCKS
RNN_CONV = 4
LRU_C = 8.0
GLA_HEADS = 4
GLA_DK = MIX_WIDTH // 2 // GLA_HEADS
GLA_DV = MIX_WIDTH // GLA_HEADS
GLA_LOW_RANK = 16
GLA_NORMALIZER = 16.0
GLA_CHUNK = 64
MEM_HEADS = 4
MEM_HEAD_DIM = MIX_WIDTH // MEM_HEADS
D_FF = 3 * D_MODEL
FFN_CONV = 3
EPS = 1e-6

IN_SPLITS = (MIX_WIDTH, GLA_HEADS * GLA_DK, GLA_HEADS * GLA_DK, GLA_HEADS * GLA_DV,
             GLA_HEADS * GLA_DV, GLA_LOW_RANK, MEM_HEADS * MEM_HEAD_DIM, N_BRANCHES * D_MODEL)
N_IN = sum(IN_SPLITS)

kernel_name = "hybrid_rglru_gla_memxattn_convffn"


def rms_norm(x, gain):
    x32 = x.astype(jnp.float32)
    y = x32 * lax.rsqrt(jnp.mean(x32 * x32, axis=-1, keepdims=True) + EPS)
    return (y * gain.astype(jnp.float32)).astype(x.dtype)


def causal_dwconv(x, w, b):
    width, ch = w.shape
    y = lax.conv_general_dilated(
        x, w[:, None, :].astype(x.dtype), window_strides=(1,), padding=[(width - 1, 0)],
        dimension_numbers=("NWC", "WIO", "NWC"), feature_group_count=ch)
    return y + b.astype(x.dtype)


def _linear_combine(left, right):
    a_l, b_l = left
    a_r, b_r = right
    return a_l * a_r, a_r * b_l + b_r


def rg_lru(xr, w_a, b_a, w_x, b_x, lam):
    bsz, slen, ch = xr.shape
    xb = xr.reshape(bsz, slen, RNN_BLOCKS, RNN_BLOCK)
    r = jax.nn.sigmoid(jnp.einsum("bshi,hij->bshj", xb, w_a) + b_a).reshape(bsz, slen, ch)
    i = jax.nn.sigmoid(jnp.einsum("bshi,hij->bshj", xb, w_x) + b_x).reshape(bsz, slen, ch)
    log_a = LRU_C * r.astype(jnp.float32) * jax.nn.log_sigmoid(lam.astype(jnp.float32))
    a = jnp.exp(log_a)
    u = jnp.sqrt(-jnp.expm1(2.0 * log_a)) * (i * xr).astype(jnp.float32)
    _, h = lax.associative_scan(_linear_combine, (a, u), axis=1)
    return h.astype(xr.dtype)


def chunked_gla(q, k, v, log_alpha):
    bsz, slen, nh, dk = q.shape
    dv = v.shape[-1]
    n_chunks = slen // GLA_CHUNK

    def to_chunks(t):
        return t.reshape(bsz, n_chunks, GLA_CHUNK, nh, t.shape[-1]).transpose(0, 1, 3, 2, 4)

    qc, kc, vc, gc = to_chunks(q), to_chunks(k), to_chunks(v), to_chunks(log_alpha)
    g_cum = jnp.cumsum(gc, axis=3)
    g_last = g_cum[:, :, :, -1:, :]
    q_e = qc * (dk ** -0.5) * jnp.exp(g_cum)
    k_e = kc * jnp.exp(-g_cum)
    k_end = kc * jnp.exp(g_last - g_cum)
    decay = jnp.exp(g_last[:, :, :, 0, :])
    mask = jnp.tril(jnp.ones((GLA_CHUNK, GLA_CHUNK), dtype=bool))
    scores = jnp.where(mask, jnp.einsum("bnhid,bnhjd->bnhij", q_e, k_e), 0.0)
    o_intra = jnp.einsum("bnhij,bnhjv->bnhiv", scores, vc)

    def step(state, xs):
        q_n, k_n, v_n, d_n = xs
        o_n = jnp.einsum("bhid,bhdv->bhiv", q_n, state)
        state = d_n[..., None] * state + jnp.einsum("bhjd,bhjv->bhdv", k_n, v_n)
        return state, o_n

    xs = (jnp.moveaxis(q_e, 1, 0), jnp.moveaxis(k_end, 1, 0), jnp.moveaxis(vc, 1, 0),
          jnp.moveaxis(decay, 1, 0))
    state0 = jnp.zeros((bsz, nh, dk, dv), jnp.float32)
    _, o_inter = lax.scan(step, state0, xs)
    o = o_intra + jnp.moveaxis(o_inter, 0, 1)
    return o.transpose(0, 1, 3, 2, 4).reshape(bsz, slen, nh, dv)


def setup_inputs(seed: int = 0) -> dict:
    key = jax.random.key(seed)
    ks = jax.random.split(key, 32)

    def nrm(k, shape, scale):
        return jax.random.normal(k, shape, jnp.float32) * scale

    def gain(k, shape):
        return 1.0 + 0.02 * jax.random.normal(k, shape, jnp.float32)

    u = jax.random.uniform(ks[10], (DEPTH, MIX_WIDTH), jnp.float32, minval=0.9, maxval=0.999)
    p = u ** (1.0 / LRU_C)
    rg_lambda = jnp.log(p) - jnp.log1p(-p)
    return {
        "x": nrm(ks[0], (BATCH, SEQ, D_MODEL), 1.0),
        "mem": nrm(ks[1], (BATCH, MEM_LEN, D_MODEL), 1.0),
        "norm_mix": gain(ks[2], (DEPTH, D_MODEL)),
        "w_in": nrm(ks[3], (DEPTH, D_MODEL, N_IN), D_MODEL ** -0.5),
        "rnn_conv_w": nrm(ks[4], (DEPTH, RNN_CONV, MIX_WIDTH), RNN_CONV ** -0.5),
        "rnn_conv_b": nrm(ks[5], (DEPTH, MIX_WIDTH), 0.01),
        "rg_w_a": nrm(ks[6], (DEPTH, RNN_BLOCKS, RNN_BLOCK, RNN_BLOCK), RNN_BLOCK ** -0.5),
        "rg_b_a": nrm(ks[7], (DEPTH, RNN_BLOCKS, RNN_BLOCK), 0.01),
        "rg_w_x": nrm(ks[8], (DEPTH, RNN_BLOCKS, RNN_BLOCK, RNN_BLOCK), RNN_BLOCK ** -0.5),
        "rg_b_x": nrm(ks[9], (DEPTH, RNN_BLOCKS, RNN_BLOCK), 0.01),
        "rg_lambda": rg_lambda,
        "gla_w_decay": nrm(ks[11], (DEPTH, GLA_LOW_RANK, GLA_HEADS * GLA_DK), GLA_LOW_RANK ** -0.5),
        "gla_b_decay": nrm(ks[12], (DEPTH, GLA_HEADS * GLA_DK), 0.01),
        "gla_norm": gain(ks[13], (DEPTH, GLA_HEADS * GLA_DV)),
        "mem_norm": gain(ks[14], (DEPTH, D_MODEL)),
        "w_mem_kv": nrm(ks[15], (DEPTH, D_MODEL, 2 * MEM_HEADS * MEM_HEAD_DIM), D_MODEL ** -0.5),
        "w_branch": nrm(ks[16], (DEPTH, N_BRANCHES, MIX_WIDTH, D_MODEL), MIX_WIDTH ** -0.5),
        "w_out": nrm(ks[17], (DEPTH, D_MODEL, D_MODEL), D_MODEL ** -0.5),
        "norm_ffn": gain(ks[18], (DEPTH, D_MODEL)),
        "w_up": nrm(ks[19], (DEPTH, D_MODEL, 2 * D_FF), D_MODEL ** -0.5),
        "ffn_conv_w": nrm(ks[20], (DEPTH, FFN_CONV, 2 * D_FF), FFN_CONV ** -0.5),
        "ffn_conv_b": nrm(ks[21], (DEPTH, 2 * D_FF), 0.01),
        "w_down": nrm(ks[22], (DEPTH, D_FF, D_MODEL), D_FF ** -0.5),
        "final_norm": gain(ks[23], (D_MODEL,)),
    }


def reference(x, mem, norm_mix, w_in, rnn_conv_w, rnn_conv_b, rg_w_a, rg_b_a, rg_w_x, rg_b_x,
              rg_lambda, gla_w_decay, gla_b_decay, gla_norm, mem_norm, w_mem_kv, w_branch, w_out,
              norm_ffn, w_up, ffn_conv_w, ffn_conv_b, w_down, final_norm):
    bsz, slen, _ = x.shape
    split_idx = np.cumsum(IN_SPLITS)[:-1].tolist()
    for l in range(DEPTH):
        h = rms_norm(x, norm_mix[l])
        z = h @ w_in[l]
        z_rnn, z_q, z_k, z_v, z_g, z_dec, z_mq, z_gate = jnp.split(z, split_idx, axis=-1)

        xr = causal_dwconv(z_rnn, rnn_conv_w[l], rnn_conv_b[l])
        out_a = rg_lru(xr, rg_w_a[l], rg_b_a[l], rg_w_x[l], rg_b_x[l], rg_lambda[l])

        dec_logit = (z_dec @ gla_w_decay[l] + gla_b_decay[l]).astype(jnp.float32)
        log_alpha = (jax.nn.log_sigmoid(dec_logit) / GLA_NORMALIZER).reshape(bsz, slen, GLA_HEADS, GLA_DK)
        o_gla = chunked_gla(
            z_q.astype(jnp.float32).reshape(bsz, slen, GLA_HEADS, GLA_DK),
            z_k.astype(jnp.float32).reshape(bsz, slen, GLA_HEADS, GLA_DK),
            z_v.astype(jnp.float32).reshape(bsz, slen, GLA_HEADS, GLA_DV),
            log_alpha).astype(x.dtype)
        o_gla = rms_norm(o_gla, gla_norm[l].reshape(GLA_HEADS, GLA_DV)).reshape(bsz, slen, -1)
        out_b = o_gla * jax.nn.silu(z_g)

        m = rms_norm(mem, mem_norm[l])
        mk, mv = jnp.split(m @ w_mem_kv[l], 2, axis=-1)
        mk = mk.reshape(bsz, MEM_LEN, MEM_HEADS, MEM_HEAD_DIM)
        mv = mv.reshape(bsz, MEM_LEN, MEM_HEADS, MEM_HEAD_DIM)
        mq = z_mq.reshape(bsz, slen, MEM_HEADS, MEM_HEAD_DIM)
        s = jnp.einsum("bshd,bmhd->bhsm", mq, mk).astype(jnp.float32) * (MEM_HEAD_DIM ** -0.5)
        prob = jax.nn.softmax(s, axis=-1).astype(mv.dtype)
        out_c = jnp.einsum("bhsm,bmhd->bshd", prob, mv).reshape(bsz, slen, MIX_WIDTH)

        branches = jnp.stack([out_a, out_b, out_c], axis=2)
        proj = jnp.einsum("bsnc,ncd->bsnd", branches, w_branch[l])
        gates = jax.nn.sigmoid(z_gate).reshape(bsz, slen, N_BRANCHES, D_MODEL)
        merged = jnp.sum(gates * proj, axis=2)
        x = x + merged @ w_out[l]

        h2 = rms_norm(x, norm_ffn[l])
        up = causal_dwconv(h2 @ w_up[l], ffn_conv_w[l], ffn_conv_b[l])
        gate_h, val_h = jnp.split(up, 2, axis=-1)
        x = x + (jax.nn.gelu(gate_h, approximate=True) * val_h) @ w_down[l]
    return rms_norm(x, final_norm)
```

```python
import functools
import math

import jax
import jax.numpy as jnp
from jax import lax
from jax.experimental import pallas as pl
from jax.experimental.pallas import tpu as pltpu

D_MODEL = 1024
MEM_LEN = 256
MIX = 768
N_BRANCHES = 3
RNN_BLOCKS = 8
RNN_BLOCK = MIX // RNN_BLOCKS
RNN_CONV = 4
LRU_C = 8.0
HEADS = 4
GLA_DK = 96
GLA_DV = 192
GLA_LOW_RANK = 16
GLA_NORMALIZER = 16.0
CHUNK = 64
CHUNK_SHIFT = CHUNK.bit_length() - 1
MEM_HEAD_DIM = 192
D_FF = 3 * D_MODEL
FFN_CONV = 3
EPS = 1e-6
IN_SPLITS = (MIX, HEADS * GLA_DK, HEADS * GLA_DK, HEADS * GLA_DV, HEADS * GLA_DV, GLA_LOW_RANK,
             HEADS * MEM_HEAD_DIM, N_BRANCHES * D_MODEL)

LANES = 128
SUBLANES = 8
DK_PAD = LANES
WIN = 2 * LANES
PAIR = 3 * LANES

C_RNN = 0
C_Q = C_RNN + MIX
C_K = C_Q + HEADS * DK_PAD
C_V = C_K + HEADS * DK_PAD
C_G = C_V + MIX
C_DEC = C_G + MIX
C_MQ = C_DEC + LANES
C_GATE = C_MQ + MIX
N_IN_PACKED = C_GATE + N_BRANCHES * D_MODEL

MIX_TILE = 256
FFN_TILE = 256
FFN_COLS = 512
VMEM_LIMIT = 56 * 1024 * 1024

BF16 = jnp.bfloat16
F32 = jnp.float32


def _dot(a, b):
    return jnp.dot(a, b, preferred_element_type=F32)


def _dot_nt(a, b):
    return lax.dot_general(a, b, (((1,), (1,)), ((), ())), preferred_element_type=F32)


def _dot_tn(a, b):
    return lax.dot_general(a, b, (((0,), (0,)), ((), ())), preferred_element_type=F32)


def _rms(x, gain):
    return x * lax.rsqrt(jnp.mean(x * x, axis=-1, keepdims=True) + EPS) * gain


def _sigmoid(x):
    return 1.0 / (1.0 + jnp.exp(-x))


def _log_sigmoid(x):
    return jnp.minimum(x, 0.0) - jnp.log1p(jnp.exp(-jnp.abs(x)))


def _head_window(h):
    pair, odd = divmod(h, 2)
    start = pair * PAIR + odd * LANES
    lo = odd * (GLA_DV - LANES)
    return start, lo, lo + GLA_DV


def _valid_lanes(h, rows):
    _, lo, hi = _head_window(h)
    lane = lax.broadcasted_iota(jnp.int32, (rows, WIN), 1)
    return (lane >= lo) & (lane < hi)


def _merge_pair(even, odd):
    return jnp.concatenate(
        [even[:, :LANES], even[:, LANES:] + odd[:, :LANES], odd[:, LANES:]], axis=-1)


def _memkv_kernel(mem_ref, gain_ref, w_ref, mk_ref, mv_ref):
    m = _rms(mem_ref[0], gain_ref[0]).astype(BF16)
    kv = _dot(m, w_ref[0])
    for h in range(HEADS):
        start, _, _ = _head_window(h)
        valid = _valid_lanes(h, MEM_LEN)
        mk_ref[0, 0, h] = jnp.where(valid, kv[:, start:start + WIN], 0.0).astype(BF16)
        mv_ref[0, 0, h] = jnp.where(valid, kv[:, MIX + start:MIX + start + WIN], 0.0).astype(BF16)


def _memkv(mem, mem_norm, w_mem_kv):
    depth = w_mem_kv.shape[0]
    bsz = mem.shape[0]
    out = jax.ShapeDtypeStruct((depth, bsz, HEADS, MEM_LEN, WIN), BF16)
    out_spec = pl.BlockSpec((1, 1, HEADS, MEM_LEN, WIN), lambda l, b: (l, b, 0, 0, 0))
    return pl.pallas_call(
        _memkv_kernel,
        grid=(depth, bsz),
        in_specs=[
            pl.BlockSpec((1, MEM_LEN, D_MODEL), lambda l, b: (b, 0, 0)),
            pl.BlockSpec((1, 1, D_MODEL), lambda l, b: (l, 0, 0)),
            pl.BlockSpec((1, D_MODEL, 2 * MIX), lambda l, b: (l, 0, 0)),
        ],
        out_specs=[out_spec, out_spec],
        out_shape=[out, out],
        compiler_params=pltpu.CompilerParams(
            dimension_semantics=("arbitrary", "arbitrary"), vmem_limit_bytes=VMEM_LIMIT),
        name="memkv",
    )(mem, mem_norm.reshape(depth, 1, D_MODEL), w_mem_kv.astype(BF16))


def _mixer_kernel(x_ref, gain_ref, w1_ref, cw_ref, cb_ref, wg_ref, bg_ref, lam_ref, wdec_ref,
                  bdec_ref, gn_ref, mk_ref, mv_ref, wb_ref, wo_ref, o_ref,
                  zc_s, sa_s, sh_s, hp_s, qe_s, ke_s, kd_s, v_s, dc_s, st_s, og_s):
    t = x_ref.shape[1]
    pad = t // 2
    j = pl.program_id(1)

    @pl.when(j == 0)
    def _():
        zc_s[0:SUBLANES, :] = jnp.zeros((SUBLANES, MIX), F32)
        hp_s[...] = jnp.zeros_like(hp_s)
        st_s[...] = jnp.zeros_like(st_s)
        sa_s[0:pad, :] = jnp.ones((pad, MIX), F32)
        sh_s[0:pad, :] = jnp.zeros((pad, MIX), F32)

    x = x_ref[0]
    hb = _rms(x, gain_ref[...]).astype(BF16)

    def proj(col, width):
        return _dot(hb, w1_ref[:, col:col + width])

    z_r = proj(C_RNN, MIX)
    zc_s[SUBLANES:SUBLANES + t, :] = z_r
    xr = cb_ref[...] + cw_ref[RNN_CONV - 1:RNN_CONV, :] * z_r
    for k in range(RNN_CONV - 1):
        off = SUBLANES - (RNN_CONV - 1) + k
        xr = xr + cw_ref[k:k + 1, :] * zc_s[off:off + t, :]
    zc_s[0:SUBLANES, :] = zc_s[t:t + SUBLANES, :]

    gates = _dot(xr.astype(BF16), wg_ref[...]) + bg_ref[...]
    r_gate = _sigmoid(gates[:, :MIX])
    i_gate = _sigmoid(gates[:, MIX:])
    log_a = (LRU_C * _log_sigmoid(lam_ref[...])) * r_gate
    a = jnp.exp(log_a)
    u = jnp.sqrt(1.0 - a * a) * (i_gate * xr)
    row = lax.broadcasted_iota(jnp.int32, (t, MIX), 0)
    u = u + jnp.where(row == 0, a * hp_s[0:1, :], 0.0)
    k = 1
    while k < t:
        sh_s[pad:pad + t, :] = u
        u_prev = sh_s[pad - k:pad - k + t, :]
        if 2 * k < t:
            sa_s[pad:pad + t, :] = a
            a_prev = sa_s[pad - k:pad - k + t, :]
            u = u + a * u_prev
            a = a * a_prev
        else:
            u = u + a * u_prev
        k *= 2
    out_a = u
    hp_s[0:1, :] = out_a[t - 1:t, :]

    q = proj(C_Q, HEADS * DK_PAD)
    kk = proj(C_K, HEADS * DK_PAD)
    v_s[...] = proj(C_V, MIX).astype(BF16)
    dec = proj(C_DEC, LANES)
    logit = _dot(dec.astype(BF16), wdec_ref[...]) + bdec_ref[...]
    g = _log_sigmoid(logit) * (1.0 / GLA_NORMALIZER)
    g_hi = g.astype(BF16)
    g_lo = (g - g_hi.astype(F32)).astype(BF16)
    ri = lax.broadcasted_iota(jnp.int32, (t, t), 0)
    ci = lax.broadcasted_iota(jnp.int32, (t, t), 1)
    same_chunk = lax.shift_right_logical(ri, CHUNK_SHIFT) == lax.shift_right_logical(ci, CHUNK_SHIFT)
    m_cum = jnp.where(same_chunk & (ci <= ri), 1.0, 0.0).astype(BF16)
    m_tot = jnp.where(same_chunk, 1.0, 0.0).astype(BF16)
    g_cum = _dot(m_cum, g_hi) + _dot(m_cum, g_lo)
    g_tot = _dot(m_tot, g_hi) + _dot(m_tot, g_lo)
    qe_s[...] = (q * ((GLA_DK ** -0.5) * jnp.exp(g_cum))).astype(BF16)
    ke_s[...] = (kk * jnp.exp(-g_cum)).astype(BF16)
    kd_s[...] = (kk * jnp.exp(g_tot - g_cum)).astype(BF16)
    dc_s[...] = jnp.exp(g_tot)

    cr = lax.broadcasted_iota(jnp.int32, (CHUNK, CHUNK), 0)
    cc = lax.broadcasted_iota(jnp.int32, (CHUNK, CHUNK), 1)
    causal = cc <= cr
    for c in range(t // CHUNK):
        rows = slice(c * CHUNK, (c + 1) * CHUNK)
        for p in range(HEADS // 2):
            outs = []
            for h in (2 * p, 2 * p + 1):
                start, _, _ = _head_window(h)
                valid = _valid_lanes(h, CHUNK)
                hc = slice(h * DK_PAD, (h + 1) * DK_PAD)
                qh = qe_s[rows, hc]
                vw = v_s[rows, start:start + WIN]
                s = jnp.where(causal, _dot_nt(qh, ke_s[rows, hc]), 0.0)
                state = st_s[h]
                o = _dot(s.astype(BF16), vw) + _dot_nt(qh, state.astype(BF16))
                st_s[h] = state * dc_s[c * CHUNK:c * CHUNK + 1, hc] + _dot_tn(vw, kd_s[rows, hc])
                ms = jnp.sum(jnp.where(valid, o * o, 0.0), axis=-1, keepdims=True) * (1.0 / GLA_DV)
                y = o * lax.rsqrt(ms + EPS) * gn_ref[:, start:start + WIN]
                outs.append(jnp.where(valid, y, 0.0))
            og_s[rows, p * PAIR:(p + 1) * PAIR] = _merge_pair(outs[0], outs[1])
    z_g = proj(C_G, MIX)
    out_b = og_s[...] * (z_g * _sigmoid(z_g))

    mq = proj(C_MQ, MIX).astype(BF16)
    parts = []
    for p in range(HEADS // 2):
        outs = []
        for h in (2 * p, 2 * p + 1):
            start, _, _ = _head_window(h)
            s = _dot_nt(mq[:, start:start + WIN], mk_ref[0, 0, h]) * (MEM_HEAD_DIM ** -0.5)
            e = jnp.exp(s - jnp.max(s, axis=-1, keepdims=True))
            inv = 1.0 / jnp.sum(e, axis=-1, keepdims=True)
            outs.append(_dot(e.astype(BF16), mv_ref[0, 0, h]) * inv)
        parts.append(_merge_pair(outs[0], outs[1]))
    out_c = jnp.concatenate(parts, axis=-1)

    merged = None
    for n, br in enumerate((out_a, out_b, out_c)):
        term = _sigmoid(proj(C_GATE + n * D_MODEL, D_MODEL)) * _dot(br.astype(BF16), wb_ref[n])
        merged = term if merged is None else merged + term
    o_ref[0] = x + _dot(merged.astype(BF16), wo_ref[...])


def _resident(shape):
    zeros = (0,) * len(shape)
    return pl.BlockSpec(shape, lambda b, j: zeros, pipeline_mode=pl.Buffered(1))


def _mixer(x, mk, mv, layer, p):
    bsz, slen, _ = x.shape
    t = min(MIX_TILE, slen)
    pad = t // 2
    tile = pl.BlockSpec((1, t, D_MODEL), lambda b, j: (b, j, 0))
    kv_spec = pl.BlockSpec((1, 1, HEADS, MEM_LEN, WIN), lambda b, j: (layer, b, 0, 0, 0))
    weights = (p["gain"], p["w1"], p["cw"], p["cb"], p["wg"], p["bg"], p["lam"], p["wdec"], p["bdec"],
               p["gn"])
    tail = (p["wb"], p["wo"])
    return pl.pallas_call(
        _mixer_kernel,
        grid=(bsz, slen // t),
        in_specs=[tile] + [_resident(w.shape) for w in weights] + [kv_spec, kv_spec]
        + [_resident(w.shape) for w in tail],
        out_specs=tile,
        out_shape=jax.ShapeDtypeStruct(x.shape, F32),
        scratch_shapes=[
            pltpu.VMEM((t + SUBLANES, MIX), F32),
            pltpu.VMEM((pad + t, MIX), F32),
            pltpu.VMEM((pad + t, MIX), F32),
            pltpu.VMEM((SUBLANES, MIX), F32),
            pltpu.VMEM((t, HEADS * DK_PAD), BF16),
            pltpu.VMEM((t, HEADS * DK_PAD), BF16),
            pltpu.VMEM((t, HEADS * DK_PAD), BF16),
            pltpu.VMEM((t, MIX), BF16),
            pltpu.VMEM((t, HEADS * DK_PAD), F32),
            pltpu.VMEM((HEADS, WIN, DK_PAD), F32),
            pltpu.VMEM((t, MIX), F32),
        ],
        compiler_params=pltpu.CompilerParams(
            dimension_semantics=("arbitrary", "arbitrary"), vmem_limit_bytes=VMEM_LIMIT),
        name="mixer",
    )(x, *weights, mk, mv, *tail)


def _gelu_tanh(x):
    return 0.5 * x * (1.0 + jnp.tanh(math.sqrt(2.0 / math.pi) * (x + 0.044715 * (x * x * x))))


def _ffn_kernel(x_ref, gain_ref, wup_ref, cw_ref, cb_ref, wdn_ref, fin_ref, o_ref, up_s, *, final):
    t = x_ref.shape[1]
    j = pl.program_id(1)

    @pl.when(j == 0)
    def _():
        up_s[0:SUBLANES, :] = jnp.zeros((SUBLANES, 2 * D_FF), F32)

    x = x_ref[0]
    hb = _rms(x, gain_ref[...]).astype(BF16)

    def conv_cols(col):
        cols = slice(col, col + FFN_COLS)
        up = _dot(hb, wup_ref[:, cols])
        up_s[SUBLANES:SUBLANES + t, cols] = up
        y = cb_ref[:, cols] + cw_ref[FFN_CONV - 1:FFN_CONV, cols] * up
        for k in range(FFN_CONV - 1):
            off = SUBLANES - (FFN_CONV - 1) + k
            y = y + cw_ref[k:k + 1, cols] * up_s[off:off + t, cols]
        return y

    acc = x
    for c in range(D_FF // FFN_COLS):
        gate_h = conv_cols(c * FFN_COLS)
        val_h = conv_cols(D_FF + c * FFN_COLS)
        act = (_gelu_tanh(gate_h) * val_h).astype(BF16)
        acc = acc + _dot(act, wdn_ref[c * FFN_COLS:(c + 1) * FFN_COLS, :])
    up_s[0:SUBLANES, :] = up_s[t:t + SUBLANES, :]
    o_ref[0] = _rms(acc, fin_ref[...]) if final else acc


def _ffn(x, p, final_gain, final):
    bsz, slen, _ = x.shape
    t = min(FFN_TILE, slen)
    tile = pl.BlockSpec((1, t, D_MODEL), lambda b, j: (b, j, 0))
    weights = (p["gain2"], p["wup"], p["cw2"], p["cb2"], p["wdn"], final_gain)
    return pl.pallas_call(
        functools.partial(_ffn_kernel, final=final),
        grid=(bsz, slen // t),
        in_specs=[tile] + [_resident(w.shape) for w in weights],
        out_specs=tile,
        out_shape=jax.ShapeDtypeStruct(x.shape, F32),
        scratch_shapes=[pltpu.VMEM((t + SUBLANES, 2 * D_FF), F32)],
        compiler_params=pltpu.CompilerParams(
            dimension_semantics=("arbitrary", "arbitrary"), vmem_limit_bytes=VMEM_LIMIT),
        name="ffn",
    )(x, *weights)


def _pad_heads(w, width, padded):
    lead = w.shape[:-1]
    w = w.reshape(*lead, HEADS, width)
    w = jnp.pad(w, [(0, 0)] * len(lead) + [(0, 0), (0, padded - width)])
    return w.reshape(*lead, HEADS * padded)


def _block_diag(w):
    eye = jnp.eye(RNN_BLOCKS, dtype=w.dtype)
    return jnp.einsum("hij,hg->higj", w, eye).reshape(MIX, MIX)


def _pack_layer(l, w_in, norm_mix, rnn_conv_w, rnn_conv_b, rg_w_a, rg_b_a, rg_w_x, rg_b_x, rg_lambda,
                gla_w_decay, gla_b_decay, gla_norm, w_branch, w_out, norm_ffn, w_up, ffn_conv_w,
                ffn_conv_b, w_down):
    bounds = [0]
    for width in IN_SPLITS:
        bounds.append(bounds[-1] + width)
    w_r, w_q, w_k, w_v, w_g, w_d, w_mq, w_gate = (
        w_in[l][:, bounds[i]:bounds[i + 1]] for i in range(len(IN_SPLITS)))
    w1 = jnp.concatenate([
        w_r, _pad_heads(w_q, GLA_DK, DK_PAD), _pad_heads(w_k, GLA_DK, DK_PAD), w_v, w_g,
        jnp.pad(w_d, ((0, 0), (0, LANES - GLA_LOW_RANK))), w_mq, w_gate], axis=1).astype(BF16)
    wdec = jnp.pad(_pad_heads(gla_w_decay[l], GLA_DK, DK_PAD), ((0, LANES - GLA_LOW_RANK), (0, 0)))
    return {
        "gain": norm_mix[l].reshape(1, D_MODEL),
        "w1": w1,
        "cw": rnn_conv_w[l],
        "cb": rnn_conv_b[l].reshape(1, MIX),
        "wg": jnp.concatenate([_block_diag(rg_w_a[l]), _block_diag(rg_w_x[l])], axis=1).astype(BF16),
        "bg": jnp.concatenate([rg_b_a[l].reshape(1, MIX), rg_b_x[l].reshape(1, MIX)], axis=1),
        "lam": rg_lambda[l].reshape(1, MIX),
        "wdec": wdec.astype(BF16),
        "bdec": _pad_heads(gla_b_decay[l].reshape(1, HEADS * GLA_DK), GLA_DK, DK_PAD),
        "gn": gla_norm[l].reshape(1, MIX),
        "wb": w_branch[l].astype(BF16),
        "wo": w_out[l].astype(BF16),
        "gain2": norm_ffn[l].reshape(1, D_MODEL),
        "wup": w_up[l].astype(BF16),
        "cw2": ffn_conv_w[l],
        "cb2": ffn_conv_b[l].reshape(1, 2 * D_FF),
        "wdn": w_down[l].astype(BF16),
    }


def kernel(x, mem, norm_mix, w_in, rnn_conv_w, rnn_conv_b, rg_w_a, rg_b_a, rg_w_x, rg_b_x, rg_lambda,
           gla_w_decay, gla_b_decay, gla_norm, mem_norm, w_mem_kv, w_branch, w_out, norm_ffn, w_up,
           ffn_conv_w, ffn_conv_b, w_down, final_norm):
    depth = w_in.shape[0]
    assert x.shape[1] % CHUNK == 0 and x.shape[2] == D_MODEL and mem.shape[1:] == (MEM_LEN, D_MODEL)
    mk, mv = _memkv(mem, mem_norm, w_mem_kv)
    final_gain = final_norm.reshape(1, D_MODEL)
    for l in range(depth):
        p = _pack_layer(l, w_in, norm_mix, rnn_conv_w, rnn_conv_b, rg_w_a, rg_b_a, rg_w_x, rg_b_x,
                        rg_lambda, gla_w_decay, gla_b_decay, gla_norm, w_branch, w_out, norm_ffn,
                        w_up, ffn_conv_w, ffn_conv_b, w_down)
        x = _mixer(x, mk, mv, l, p)
        x = _ffn(x, p, final_gain, final=(l == depth - 1))
    return x
```

```python
import functools
import math

import jax
import jax.numpy as jnp
from jax import lax
from jax.experimental import pallas as pl
from jax.experimental.pallas import tpu as pltpu

D_MODEL = 1024
MEM_LEN = 256
MIX = 768
N_BRANCHES = 3
RNN_BLOCKS = 8
RNN_BLOCK = MIX // RNN_BLOCKS
RNN_GROUPS = 2
RNN_GROUP = MIX // RNN_GROUPS
RNN_CONV = 4
LRU_C = 8.0
HEADS = 4
GLA_DK = 96
GLA_DV = 192
GLA_LOW_RANK = 16
GLA_NORMALIZER = 16.0
CHUNK = 64
CHUNK_SHIFT = CHUNK.bit_length() - 1
MEM_HEAD_DIM = 192
D_FF = 3 * D_MODEL
FFN_CONV = 3
EPS = 1e-6
IN_SPLITS = (MIX, HEADS * GLA_DK, HEADS * GLA_DK, HEADS * GLA_DV, HEADS * GLA_DV, GLA_LOW_RANK,
             HEADS * MEM_HEAD_DIM, N_BRANCHES * D_MODEL)

LANES = 128
SUBLANES = 8
DK_PAD = LANES
WIN = 2 * LANES
PAIR = 3 * LANES

C_RNN = 0
C_Q = C_RNN + MIX
C_K = C_Q + HEADS * DK_PAD
C_V = C_K + HEADS * DK_PAD
C_G = C_V + MIX
C_MQ = C_G + MIX
C_GATE = C_MQ + MIX
N_IN_PACKED = C_GATE + N_BRANCHES * D_MODEL

MIX_TILE = 256
FFN_TILE = 512
FFN_COLS = 3072
VMEM_LIMIT = 56 * 1024 * 1024

BF16 = jnp.bfloat16
F32 = jnp.float32


def _dot(a, b):
    return jnp.dot(a, b, preferred_element_type=F32)


def _dot_nt(a, b):
    return lax.dot_general(a, b, (((1,), (1,)), ((), ())), preferred_element_type=F32)


def _dot_tn(a, b):
    return lax.dot_general(a, b, (((0,), (0,)), ((), ())), preferred_element_type=F32)


def _rms(x, gain):
    return x * lax.rsqrt(jnp.mean(x * x, axis=-1, keepdims=True) + EPS) * gain


def _sigmoid(x):
    return 1.0 / (1.0 + jnp.exp(-x))


def _log_sigmoid(x):
    return jnp.minimum(x, 0.0) - jnp.log1p(jnp.exp(-jnp.abs(x)))


def _log_sigmoid_wide(x):
    return jnp.minimum(x, 0.0) - jnp.log(1.0 + jnp.exp(-jnp.abs(x)))


def _sqrt_nonneg(y):
    return jnp.where(y > 0.0, y * lax.rsqrt(y), 0.0)


def _shift_rows(x, tail, k):
    rolled = pltpu.roll(x, k, 0)
    row = lax.broadcasted_iota(jnp.int32, (SUBLANES, x.shape[1]), 0)
    top = jnp.where(row < k, pltpu.roll(tail, k, 0), rolled[:SUBLANES])
    return jnp.concatenate([top, rolled[SUBLANES:]], axis=0)


def _head_window(h):
    pair, odd = divmod(h, 2)
    start = pair * PAIR + odd * LANES
    lo = odd * (GLA_DV - LANES)
    return start, lo, lo + GLA_DV


def _valid_lanes(h, rows):
    _, lo, hi = _head_window(h)
    lane = lax.broadcasted_iota(jnp.int32, (rows, WIN), 1)
    return (lane >= lo) & (lane < hi)


def _merge_pair(even, odd):
    return jnp.concatenate(
        [even[:, :LANES], even[:, LANES:] + odd[:, :LANES], odd[:, LANES:]], axis=-1)


def _memkv_kernel(mem_ref, gain_ref, w_ref, mk_ref, mv_ref):
    m = _rms(mem_ref[0], gain_ref[0]).astype(BF16)
    kv = _dot(m, w_ref[0])
    for h in range(HEADS):
        start, _, _ = _head_window(h)
        valid = _valid_lanes(h, MEM_LEN)
        mk_ref[0, 0, h] = jnp.where(valid, kv[:, start:start + WIN], 0.0).astype(BF16)
        mv_ref[0, 0, h] = jnp.where(valid, kv[:, MIX + start:MIX + start + WIN], 0.0).astype(BF16)


def _memkv(mem, mem_norm, w_mem_kv):
    depth = w_mem_kv.shape[0]
    bsz = mem.shape[0]
    out = jax.ShapeDtypeStruct((depth, bsz, HEADS, MEM_LEN, WIN), BF16)
    out_spec = pl.BlockSpec((1, 1, HEADS, MEM_LEN, WIN), lambda l, b: (l, b, 0, 0, 0))
    return pl.pallas_call(
        _memkv_kernel,
        grid=(depth, bsz),
        in_specs=[
            pl.BlockSpec((1, MEM_LEN, D_MODEL), lambda l, b: (b, 0, 0)),
            pl.BlockSpec((1, 1, D_MODEL), lambda l, b: (l, 0, 0)),
            pl.BlockSpec((1, D_MODEL, 2 * MIX), lambda l, b: (l, 0, 0)),
        ],
        out_specs=[out_spec, out_spec],
        out_shape=[out, out],
        compiler_params=pltpu.CompilerParams(
            dimension_semantics=("arbitrary", "arbitrary"), vmem_limit_bytes=VMEM_LIMIT),
        name="memkv",
    )(mem, mem_norm.reshape(depth, 1, D_MODEL), w_mem_kv.astype(BF16))


def _mixer_kernel(x_ref, gain_ref, w1_ref, cw_ref, cb_ref, wg_ref, bg_ref, lam_ref, wdec_ref,
                  bdec_ref, gn_ref, mk_ref, mv_ref, wb_ref, wo_ref, o_ref,
                  zt_s, sa_s, sh_s, hp_s, st_s):
    t = x_ref.shape[1]
    pad = t // 2
    j = pl.program_id(1)

    @pl.when(j == 0)
    def _():
        zt_s[...] = jnp.zeros_like(zt_s)
        hp_s[...] = jnp.zeros_like(hp_s)
        st_s[...] = jnp.zeros_like(st_s)
        sa_s[0:pad, :] = jnp.ones((pad, MIX), F32)
        sh_s[0:pad, :] = jnp.zeros((pad, MIX), F32)

    x = x_ref[0]
    hb = _rms(x, gain_ref[...]).astype(BF16)

    def proj(col, width):
        return _dot(hb, w1_ref[:, col:col + width])

    z_r = proj(C_RNN, MIX)
    tail = zt_s[...]
    xr = cb_ref[...] + cw_ref[RNN_CONV - 1:RNN_CONV, :] * z_r
    for k in range(1, RNN_CONV):
        xr = xr + cw_ref[RNN_CONV - 1 - k:RNN_CONV - k, :] * _shift_rows(z_r, tail, k)
    zt_s[...] = z_r[t - SUBLANES:, :]

    xr_b = xr.astype(BF16)
    r_parts, i_parts = [], []
    for grp in range(RNN_GROUPS):
        lanes = slice(grp * RNN_GROUP, (grp + 1) * RNN_GROUP)
        gates = _dot(xr_b[:, lanes], wg_ref[grp]) + bg_ref[grp]
        r_parts.append(gates[:, :RNN_GROUP])
        i_parts.append(gates[:, RNN_GROUP:])
    r_gate = _sigmoid(jnp.concatenate(r_parts, axis=-1))
    i_gate = _sigmoid(jnp.concatenate(i_parts, axis=-1))
    log_a = (LRU_C * _log_sigmoid(lam_ref[...])) * r_gate
    a = jnp.exp(log_a)
    u = _sqrt_nonneg(1.0 - a * a) * (i_gate * xr)
    row = lax.broadcasted_iota(jnp.int32, (t, MIX), 0)
    u = u + jnp.where(row == 0, a * hp_s[0:1, :], 0.0)
    k = 1
    while k < t:
        if k < SUBLANES:
            keep = row >= k
            u_prev = jnp.where(keep, pltpu.roll(u, k, 0), 0.0)
            a_prev = jnp.where(keep, pltpu.roll(a, k, 0), 1.0)
        else:
            sh_s[pad:pad + t, :] = u
            u_prev = sh_s[pad - k:pad - k + t, :]
            if 2 * k < t:
                sa_s[pad:pad + t, :] = a
                a_prev = sa_s[pad - k:pad - k + t, :]
        u = u + a * u_prev
        if 2 * k < t:
            a = a * a_prev
        k *= 2
    out_a = u
    hp_s[0:1, :] = out_a[t - 1:t, :]

    q = proj(C_Q, HEADS * DK_PAD)
    kk = proj(C_K, HEADS * DK_PAD)
    v = proj(C_V, MIX).astype(BF16)
    logit = _dot(q[:, :DK_PAD].astype(BF16), wdec_ref[...]) + bdec_ref[...]
    g = _log_sigmoid_wide(logit) * (1.0 / GLA_NORMALIZER)
    g_hi = g.astype(BF16)
    g_lo = (g - g_hi.astype(F32)).astype(BF16)
    ri = lax.broadcasted_iota(jnp.int32, (t, t), 0)
    ci = lax.broadcasted_iota(jnp.int32, (t, t), 1)
    same_chunk = lax.shift_right_logical(ri, CHUNK_SHIFT) == lax.shift_right_logical(ci, CHUNK_SHIFT)
    m_cum = jnp.where(same_chunk & (ci <= ri), 1.0, 0.0).astype(BF16)
    m_tot = jnp.where(same_chunk, 1.0, 0.0).astype(BF16)
    g_cum = _dot(m_cum, g_hi) + _dot(m_cum, g_lo)
    g_tot = _dot(m_tot, g_hi) + _dot(m_tot, g_lo)
    qe = (q * ((GLA_DK ** -0.5) * jnp.exp(g_cum))).astype(BF16)
    ke = (kk * jnp.exp(-g_cum)).astype(BF16)
    kd = (kk * jnp.exp(g_tot - g_cum)).astype(BF16)
    decay = jnp.exp(g_tot)

    n_chunks = t // CHUNK
    cr = lax.broadcasted_iota(jnp.int32, (CHUNK, CHUNK), 0)
    cc = lax.broadcasted_iota(jnp.int32, (CHUNK, CHUNK), 1)
    causal = cc <= cr

    def rows(c):
        return slice(c * CHUNK, (c + 1) * CHUNK)

    def head_cols(h):
        return slice(h * DK_PAD, (h + 1) * DK_PAD)

    def window(h):
        start, _, _ = _head_window(h)
        return slice(start, start + WIN)

    scores = [[jnp.where(causal, _dot_nt(qe[rows(c), head_cols(h)], ke[rows(c), head_cols(h)]), 0.0)
               .astype(BF16) for h in range(HEADS)] for c in range(n_chunks)]
    kv_new = [[_dot_tn(v[rows(c), window(h)], kd[rows(c), head_cols(h)]) for h in range(HEADS)]
              for c in range(n_chunks)]
    state_in = [[None] * HEADS for _ in range(n_chunks)]
    for h in range(HEADS):
        state = st_s[h]
        for c in range(n_chunks):
            state_in[c][h] = state.astype(BF16)
            state = state * decay[c * CHUNK:c * CHUNK + 1, head_cols(h)] + kv_new[c][h]
        st_s[h] = state
    chunk_out = []
    for c in range(n_chunks):
        pairs = []
        for p in range(HEADS // 2):
            outs = []
            for h in (2 * p, 2 * p + 1):
                valid = _valid_lanes(h, CHUNK)
                o = (_dot(scores[c][h], v[rows(c), window(h)])
                     + _dot_nt(qe[rows(c), head_cols(h)], state_in[c][h]))
                ms = jnp.sum(jnp.where(valid, o * o, 0.0), axis=-1, keepdims=True) * (1.0 / GLA_DV)
                y = o * lax.rsqrt(ms + EPS) * gn_ref[:, window(h)]
                outs.append(jnp.where(valid, y, 0.0))
            pairs.append(_merge_pair(outs[0], outs[1]))
        chunk_out.append(jnp.concatenate(pairs, axis=-1))
    z_g = proj(C_G, MIX)
    out_b = jnp.concatenate(chunk_out, axis=0) * (z_g * _sigmoid(z_g))

    mq = proj(C_MQ, MIX).astype(BF16)
    parts = []
    for p in range(HEADS // 2):
        outs = []
        for h in (2 * p, 2 * p + 1):
            s = _dot_nt(mq[:, window(h)], mk_ref[0, 0, h]) * (MEM_HEAD_DIM ** -0.5)
            e = jnp.exp(s - jnp.max(s, axis=-1, keepdims=True))
            inv = 1.0 / jnp.sum(e, axis=-1, keepdims=True)
            outs.append(_dot(e.astype(BF16), mv_ref[0, 0, h]) * inv)
        parts.append(_merge_pair(outs[0], outs[1]))
    out_c = jnp.concatenate(parts, axis=-1)

    merged = None
    for n, br in enumerate((out_a, out_b, out_c)):
        term = _sigmoid(proj(C_GATE + n * D_MODEL, D_MODEL)) * _dot(br.astype(BF16), wb_ref[n])
        merged = term if merged is None else merged + term
    o_ref[0] = x + _dot(merged.astype(BF16), wo_ref[...])


def _resident(shape):
    zeros = (0,) * len(shape)
    return pl.BlockSpec(shape, lambda b, j: zeros, pipeline_mode=pl.Buffered(1))


def _mixer(x, mk, mv, layer, p):
    bsz, slen, _ = x.shape
    t = min(MIX_TILE, slen)
    pad = t // 2
    tile = pl.BlockSpec((1, t, D_MODEL), lambda b, j: (b, j, 0))
    kv_spec = pl.BlockSpec((1, 1, HEADS, MEM_LEN, WIN), lambda b, j: (layer, b, 0, 0, 0))
    weights = (p["gain"], p["w1"], p["cw"], p["cb"], p["wg"], p["bg"], p["lam"], p["wdec"], p["bdec"],
               p["gn"])
    tail = (p["wb"], p["wo"])
    return pl.pallas_call(
        _mixer_kernel,
        grid=(bsz, slen // t),
        in_specs=[tile] + [_resident(w.shape) for w in weights] + [kv_spec, kv_spec]
        + [_resident(w.shape) for w in tail],
        out_specs=tile,
        out_shape=jax.ShapeDtypeStruct(x.shape, F32),
        scratch_shapes=[
            pltpu.VMEM((SUBLANES, MIX), F32),
            pltpu.VMEM((pad + t, MIX), F32),
            pltpu.VMEM((pad + t, MIX), F32),
            pltpu.VMEM((SUBLANES, MIX), F32),
            pltpu.VMEM((HEADS, WIN, DK_PAD), F32),
        ],
        compiler_params=pltpu.CompilerParams(
            dimension_semantics=("arbitrary", "arbitrary"), vmem_limit_bytes=VMEM_LIMIT),
        name="mixer",
    )(x, *weights, mk, mv, *tail)


GELU_C0 = math.sqrt(2.0 / math.pi)
GELU_C1 = GELU_C0 * 0.044715


def _ffn_kernel(x_ref, gain_ref, wup_ref, cw_ref, cb_ref, wdn_ref, fin_ref, o_ref, ut_s, *, final):
    t = x_ref.shape[1]
    j = pl.program_id(1)

    @pl.when(j == 0)
    def _():
        ut_s[...] = jnp.zeros_like(ut_s)

    x = x_ref[0]
    hb = _rms(x, gain_ref[...]).astype(BF16)

    def conv_cols(col, scale):
        cols = slice(col, col + FFN_COLS)
        up = _dot(hb, wup_ref[:, cols])
        tail = ut_s[:, cols]
        cw = cw_ref[:, cols] * scale
        y = cb_ref[:, cols] * scale + cw[FFN_CONV - 1:FFN_CONV, :] * up
        for k in range(1, FFN_CONV):
            y = y + cw[FFN_CONV - 1 - k:FFN_CONV - k, :] * _shift_rows(up, tail, k)
        ut_s[:, cols] = up[t - SUBLANES:, :]
        return y

    acc = x
    for c in range(D_FF // FFN_COLS):
        gate_h = conv_cols(c * FFN_COLS, 1.0)
        half_val = conv_cols(D_FF + c * FFN_COLS, 0.5)
        th = jnp.tanh(gate_h * (GELU_C0 + GELU_C1 * (gate_h * gate_h)))
        act = ((gate_h + gate_h * th) * half_val).astype(BF16)
        acc = acc + _dot(act, wdn_ref[c * FFN_COLS:(c + 1) * FFN_COLS, :])
    o_ref[0] = _rms(acc, fin_ref[...]) if final else acc


def _ffn(x, p, final_gain, final):
    bsz, slen, _ = x.shape
    t = min(FFN_TILE, slen)
    tile = pl.BlockSpec((1, t, D_MODEL), lambda b, j: (b, j, 0))
    weights = (p["gain2"], p["wup"], p["cw2"], p["cb2"], p["wdn"], final_gain)
    return pl.pallas_call(
        functools.partial(_ffn_kernel, final=final),
        grid=(bsz, slen // t),
        in_specs=[tile] + [_resident(w.shape) for w in weights],
        out_specs=tile,
        out_shape=jax.ShapeDtypeStruct(x.shape, F32),
        scratch_shapes=[pltpu.VMEM((SUBLANES, 2 * D_FF), F32)],
        compiler_params=pltpu.CompilerParams(
            dimension_semantics=("arbitrary", "arbitrary"), vmem_limit_bytes=VMEM_LIMIT),
        name="ffn",
    )(x, *weights)


def _pad_heads(w, width, padded):
    lead = w.shape[:-1]
    w = w.reshape(*lead, HEADS, width)
    w = jnp.pad(w, [(0, 0)] * len(lead) + [(0, 0), (0, padded - width)])
    return w.reshape(*lead, HEADS * padded)


def _block_diag(w):
    n, rows, cols = w.shape
    eye = jnp.eye(n, dtype=w.dtype)
    return jnp.einsum("hij,hg->higj", w, eye).reshape(n * rows, n * cols)


def _gate_weights(w_a, w_x):
    per = RNN_BLOCKS // RNN_GROUPS
    return jnp.stack([
        jnp.concatenate([_block_diag(w_a[g * per:(g + 1) * per]),
                         _block_diag(w_x[g * per:(g + 1) * per])], axis=1)
        for g in range(RNN_GROUPS)])


def _pack_layer(l, w_in, norm_mix, rnn_conv_w, rnn_conv_b, rg_w_a, rg_b_a, rg_w_x, rg_b_x, rg_lambda,
                gla_w_decay, gla_b_decay, gla_norm, w_branch, w_out, norm_ffn, w_up, ffn_conv_w,
                ffn_conv_b, w_down):
    bounds = [0]
    for width in IN_SPLITS:
        bounds.append(bounds[-1] + width)
    w_r, w_q, w_k, w_v, w_g, w_d, w_mq, w_gate = (
        w_in[l][:, bounds[i]:bounds[i + 1]] for i in range(len(IN_SPLITS)))
    w_q = _pad_heads(w_q, GLA_DK, DK_PAD)
    w_q = w_q.at[:, GLA_DK:GLA_DK + GLA_LOW_RANK].set(w_d)
    w1 = jnp.concatenate(
        [w_r, w_q, _pad_heads(w_k, GLA_DK, DK_PAD), w_v, w_g, w_mq, w_gate], axis=1).astype(BF16)
    wdec = jnp.zeros((DK_PAD, HEADS * DK_PAD), F32).at[GLA_DK:GLA_DK + GLA_LOW_RANK, :].set(
        _pad_heads(gla_w_decay[l], GLA_DK, DK_PAD))
    b_a = rg_b_a[l].reshape(RNN_GROUPS, 1, RNN_GROUP)
    b_x = rg_b_x[l].reshape(RNN_GROUPS, 1, RNN_GROUP)
    return {
        "gain": norm_mix[l].reshape(1, D_MODEL),
        "w1": w1,
        "cw": rnn_conv_w[l],
        "cb": rnn_conv_b[l].reshape(1, MIX),
        "wg": _gate_weights(rg_w_a[l], rg_w_x[l]).astype(BF16),
        "bg": jnp.concatenate([b_a, b_x], axis=-1),
        "lam": rg_lambda[l].reshape(1, MIX),
        "wdec": wdec.astype(BF16),
        "bdec": _pad_heads(gla_b_decay[l].reshape(1, HEADS * GLA_DK), GLA_DK, DK_PAD),
        "gn": gla_norm[l].reshape(1, MIX),
        "wb": w_branch[l].astype(BF16),
        "wo": w_out[l].astype(BF16),
        "gain2": norm_ffn[l].reshape(1, D_MODEL),
        "wup": w_up[l].astype(BF16),
        "cw2": ffn_conv_w[l],
        "cb2": ffn_conv_b[l].reshape(1, 2 * D_FF),
        "wdn": w_down[l].astype(BF16),
    }


def kernel(x, mem, norm_mix, w_in, rnn_conv_w, rnn_conv_b, rg_w_a, rg_b_a, rg_w_x, rg_b_x, rg_lambda,
           gla_w_decay, gla_b_decay, gla_norm, mem_norm, w_mem_kv, w_branch, w_out, norm_ffn, w_up,
           ffn_conv_w, ffn_conv_b, w_down, final_norm):
    depth = w_in.shape[0]
    assert x.shape[1] % CHUNK == 0 and x.shape[2] == D_MODEL and mem.shape[1:] == (MEM_LEN, D_MODEL)
    mk, mv = _memkv(mem, mem_norm, w_mem_kv)
    final_gain = final_norm.reshape(1, D_MODEL)
    for l in range(depth):
        p = _pack_layer(l, w_in, norm_mix, rnn_conv_w, rnn_conv_b, rg_w_a, rg_b_a, rg_w_x, rg_b_x,
                        rg_lambda, gla_w_decay, gla_b_decay, gla_norm, w_branch, w_out, norm_ffn,
                        w_up, ffn_conv_w, ffn_conv_b, w_down)
        x = _mixer(x, mk, mv, l, p)
        x = _ffn(x, p, final_gain, final=(l == depth - 1))
    return x
```

```python
import functools
import math

import jax
import jax.numpy as jnp
from jax import lax
from jax.experimental import pallas as pl
from jax.experimental.pallas import tpu as pltpu

D_MODEL = 1024
MEM_LEN = 256
MIX = 768
N_BRANCHES = 3
RNN_BLOCKS = 8
RNN_BLOCK = MIX // RNN_BLOCKS
RNN_GROUPS = 2
RNN_GROUP = MIX // RNN_GROUPS
RNN_CONV = 4
LRU_C = 8.0
HEADS = 4
GLA_DK = 96
GLA_DV = 192
GLA_LOW_RANK = 16
GLA_NORMALIZER = 16.0
CHUNK = 64
CHUNK_SHIFT = CHUNK.bit_length() - 1
MEM_HEAD_DIM = 192
D_FF = 3 * D_MODEL
FFN_CONV = 3
EPS = 1e-6
IN_SPLITS = (MIX, HEADS * GLA_DK, HEADS * GLA_DK, HEADS * GLA_DV, HEADS * GLA_DV, GLA_LOW_RANK,
             HEADS * MEM_HEAD_DIM, N_BRANCHES * D_MODEL)

LANES = 128
SUBLANES = 8
DK_PAD = LANES
WIN = 2 * LANES
PAIR = 3 * LANES

C_RNN = 0
C_Q = C_RNN + MIX
C_K = C_Q + HEADS * DK_PAD
C_V = C_K + HEADS * DK_PAD
C_G = C_V + MIX
C_MQ = C_G + MIX
C_GATE = C_MQ + MIX
N_IN_PACKED = C_GATE + N_BRANCHES * D_MODEL

MIX_TILE = 512
FFN_TILE = 512
FFN_COLS = 3072
VMEM_LIMIT = 56 * 1024 * 1024

BF16 = jnp.bfloat16
F32 = jnp.float32


def _dot(a, b):
    return jnp.dot(a, b, preferred_element_type=F32)


def _dot_nt(a, b):
    return lax.dot_general(a, b, (((1,), (1,)), ((), ())), preferred_element_type=F32)


def _dot_tn(a, b):
    return lax.dot_general(a, b, (((0,), (0,)), ((), ())), preferred_element_type=F32)


def _rms(x, gain):
    return x * lax.rsqrt(jnp.mean(x * x, axis=-1, keepdims=True) + EPS) * gain


def _sigmoid(x):
    return 1.0 / (1.0 + jnp.exp(-x))


def _log_sigmoid(x):
    return jnp.minimum(x, 0.0) - jnp.log1p(jnp.exp(-jnp.abs(x)))


def _log_sigmoid_wide(x):
    return jnp.minimum(x, 0.0) - jnp.log(1.0 + jnp.exp(-jnp.abs(x)))


def _sqrt_nonneg(y):
    return jnp.where(y > 0.0, y * lax.rsqrt(y), 0.0)


def _shift_rows(x, tail, k):
    rolled = pltpu.roll(x, k, 0)
    row = lax.broadcasted_iota(jnp.int32, (SUBLANES, x.shape[1]), 0)
    top = jnp.where(row < k, pltpu.roll(tail, k, 0), rolled[:SUBLANES])
    return jnp.concatenate([top, rolled[SUBLANES:]], axis=0)


def _head_window(h):
    pair, odd = divmod(h, 2)
    start = pair * PAIR + odd * LANES
    lo = odd * (GLA_DV - LANES)
    return start, lo, lo + GLA_DV


def _valid_lanes(h, rows):
    _, lo, hi = _head_window(h)
    lane = lax.broadcasted_iota(jnp.int32, (rows, WIN), 1)
    return (lane >= lo) & (lane < hi)


def _merge_pair(even, odd):
    return jnp.concatenate(
        [even[:, :LANES], even[:, LANES:] + odd[:, :LANES], odd[:, LANES:]], axis=-1)


def _memkv_kernel(mem_ref, gain_ref, w_ref, mk_ref, mv_ref):
    m = _rms(mem_ref[0], gain_ref[0]).astype(BF16)
    kv = _dot(m, w_ref[0])
    for h in range(HEADS):
        start, _, _ = _head_window(h)
        valid = _valid_lanes(h, MEM_LEN)
        mk_ref[0, 0, h] = jnp.where(valid, kv[:, start:start + WIN], 0.0).astype(BF16)
        mv_ref[0, 0, h] = jnp.where(valid, kv[:, MIX + start:MIX + start + WIN], 0.0).astype(BF16)


def _memkv(mem, mem_norm, w_mem_kv):
    depth = w_mem_kv.shape[0]
    bsz = mem.shape[0]
    out = jax.ShapeDtypeStruct((depth, bsz, HEADS, MEM_LEN, WIN), BF16)
    out_spec = pl.BlockSpec((1, 1, HEADS, MEM_LEN, WIN), lambda l, b: (l, b, 0, 0, 0))
    return pl.pallas_call(
        _memkv_kernel,
        grid=(depth, bsz),
        in_specs=[
            pl.BlockSpec((1, MEM_LEN, D_MODEL), lambda l, b: (b, 0, 0)),
            pl.BlockSpec((1, 1, D_MODEL), lambda l, b: (l, 0, 0)),
            pl.BlockSpec((1, D_MODEL, 2 * MIX), lambda l, b: (l, 0, 0)),
        ],
        out_specs=[out_spec, out_spec],
        out_shape=[out, out],
        compiler_params=pltpu.CompilerParams(
            dimension_semantics=("arbitrary", "arbitrary"), vmem_limit_bytes=VMEM_LIMIT),
        name="memkv",
    )(mem, mem_norm.reshape(depth, 1, D_MODEL), w_mem_kv.astype(BF16))


def _mixer_kernel(x_ref, gain_ref, w1_ref, cw_ref, cb_ref, wg_ref, bg_ref, lam_ref, wdec_ref,
                  bdec_ref, gn_ref, mk_ref, mv_ref, wb_ref, wo_ref, o_ref,
                  zt_s, sa_s, sh_s, hp_s, st_s):
    t = x_ref.shape[1]
    pad = t // 2
    j = pl.program_id(1)

    @pl.when(j == 0)
    def _():
        zt_s[...] = jnp.zeros_like(zt_s)
        hp_s[...] = jnp.zeros_like(hp_s)
        st_s[...] = jnp.zeros_like(st_s)
        sa_s[0:pad, :] = jnp.ones((pad, MIX), F32)
        sh_s[0:pad, :] = jnp.zeros((pad, MIX), F32)

    x = x_ref[0]
    hb = _rms(x, gain_ref[...]).astype(BF16)

    def proj(col, width):
        return _dot(hb, w1_ref[:, col:col + width])

    queue = [col + i * WIN for col, width in ((C_Q, HEADS * DK_PAD), (C_K, HEADS * DK_PAD), (C_V, MIX),
                                              (C_MQ, MIX), (C_G, MIX), (C_GATE, N_BRANCHES * D_MODEL))
             for i in range(width // WIN)]
    ready = {}

    def fill(n):
        for _ in range(min(n, len(queue))):
            col = queue.pop(0)
            ready[col] = proj(col, WIN)

    def take(col, width):
        cols = [col + i * WIN for i in range(width // WIN)]
        while any(c not in ready for c in cols):
            fill(1)
        return jnp.concatenate([ready.pop(c) for c in cols], axis=-1)

    z_r = proj(C_RNN, MIX)
    tail = zt_s[...]
    xr = cb_ref[...] + cw_ref[RNN_CONV - 1:RNN_CONV, :] * z_r
    for k in range(1, RNN_CONV):
        xr = xr + cw_ref[RNN_CONV - 1 - k:RNN_CONV - k, :] * _shift_rows(z_r, tail, k)
        fill(k % 2)
    zt_s[...] = z_r[t - SUBLANES:, :]

    xr_b = xr.astype(BF16)
    r_parts, i_parts = [], []
    for grp in range(RNN_GROUPS):
        lanes = slice(grp * RNN_GROUP, (grp + 1) * RNN_GROUP)
        gates = _dot(xr_b[:, lanes], wg_ref[grp]) + bg_ref[grp]
        r_parts.append(gates[:, :RNN_GROUP])
        i_parts.append(gates[:, RNN_GROUP:])
    r_gate = _sigmoid(jnp.concatenate(r_parts, axis=-1))
    fill(1)
    i_gate = _sigmoid(jnp.concatenate(i_parts, axis=-1))
    fill(1)
    log_a = (LRU_C * _log_sigmoid(lam_ref[...])) * r_gate
    a = jnp.exp(log_a)
    u = _sqrt_nonneg(1.0 - a * a) * (i_gate * xr)
    fill(1)
    row = lax.broadcasted_iota(jnp.int32, (t, MIX), 0)
    u = u + jnp.where(row == 0, a * hp_s[0:1, :], 0.0)
    k = 1
    while k < t:
        if k < SUBLANES:
            keep = row >= k
            u_prev = jnp.where(keep, pltpu.roll(u, k, 0), 0.0)
            a_prev = jnp.where(keep, pltpu.roll(a, k, 0), 1.0)
        else:
            sh_s[pad:pad + t, :] = u
            u_prev = sh_s[pad - k:pad - k + t, :]
            if 2 * k < t:
                sa_s[pad:pad + t, :] = a
                a_prev = sa_s[pad - k:pad - k + t, :]
        u = u + a * u_prev
        if 2 * k < t:
            a = a * a_prev
        fill(1 if k < SUBLANES or k in (16, 64) else 0)
        k *= 2
    out_a = u
    hp_s[0:1, :] = out_a[t - 1:t, :]

    q = take(C_Q, HEADS * DK_PAD)
    kk = take(C_K, HEADS * DK_PAD)
    v = take(C_V, MIX).astype(BF16)
    logit = _dot(q[:, :DK_PAD].astype(BF16), wdec_ref[...]) + bdec_ref[...]
    g = _log_sigmoid_wide(logit) * (1.0 / GLA_NORMALIZER)
    g_hi = g.astype(BF16)
    g_lo = (g - g_hi.astype(F32)).astype(BF16)
    ri = lax.broadcasted_iota(jnp.int32, (t, t), 0)
    ci = lax.broadcasted_iota(jnp.int32, (t, t), 1)
    same_chunk = lax.shift_right_logical(ri, CHUNK_SHIFT) == lax.shift_right_logical(ci, CHUNK_SHIFT)
    m_cum = jnp.where(same_chunk & (ci <= ri), 1.0, 0.0).astype(BF16)
    g_cum = _dot(m_cum, g_hi) + _dot(m_cum, g_lo)
    n_chunks = t // CHUNK
    g_tot = jnp.concatenate(
        [jnp.broadcast_to(g_cum[(c + 1) * CHUNK - 1:(c + 1) * CHUNK, :], (CHUNK, HEADS * DK_PAD))
         for c in range(n_chunks)], axis=0)
    qe = (q * ((GLA_DK ** -0.5) * jnp.exp(g_cum))).astype(BF16)
    fill(1)
    ke = (kk * jnp.exp(-g_cum)).astype(BF16)
    fill(1)
    kd = (kk * jnp.exp(g_tot - g_cum)).astype(BF16)
    fill(1)
    decay = jnp.exp(g_tot)

    cr = lax.broadcasted_iota(jnp.int32, (CHUNK, CHUNK), 0)
    cc = lax.broadcasted_iota(jnp.int32, (CHUNK, CHUNK), 1)
    causal = cc <= cr

    def rows(c):
        return slice(c * CHUNK, (c + 1) * CHUNK)

    def head_cols(h):
        return slice(h * DK_PAD, (h + 1) * DK_PAD)

    def window(h):
        start, _, _ = _head_window(h)
        return slice(start, start + WIN)

    scores = [[jnp.where(causal, _dot_nt(qe[rows(c), head_cols(h)], ke[rows(c), head_cols(h)]), 0.0)
               .astype(BF16) for h in range(HEADS)] for c in range(n_chunks)]
    kv_new = [[_dot_tn(v[rows(c), window(h)], kd[rows(c), head_cols(h)]) for h in range(HEADS)]
              for c in range(n_chunks)]
    state_in = [[None] * HEADS for _ in range(n_chunks)]
    for h in range(HEADS):
        state = st_s[h]
        for c in range(n_chunks):
            state_in[c][h] = state.astype(BF16)
            state = state * decay[c * CHUNK:c * CHUNK + 1, head_cols(h)] + kv_new[c][h]
        st_s[h] = state
    chunk_out = []
    for c in range(n_chunks):
        pairs = []
        for p in range(HEADS // 2):
            outs = []
            for h in (2 * p, 2 * p + 1):
                valid = _valid_lanes(h, CHUNK)
                o = (_dot(scores[c][h], v[rows(c), window(h)])
                     + _dot_nt(qe[rows(c), head_cols(h)], state_in[c][h]))
                ms = jnp.sum(jnp.where(valid, o * o, 0.0), axis=-1, keepdims=True) * (1.0 / GLA_DV)
                y = o * lax.rsqrt(ms + EPS) * gn_ref[:, window(h)]
                outs.append(jnp.where(valid, y, 0.0))
            pairs.append(_merge_pair(outs[0], outs[1]))
        chunk_out.append(jnp.concatenate(pairs, axis=-1))
        fill(1)
    z_g = take(C_G, MIX)
    out_b = jnp.concatenate(chunk_out, axis=0) * (z_g * _sigmoid(z_g))

    mq = take(C_MQ, MIX).astype(BF16)
    parts = []
    for p in range(HEADS // 2):
        outs = []
        for h in (2 * p, 2 * p + 1):
            s = _dot_nt(mq[:, window(h)], mk_ref[0, 0, h]) * (MEM_HEAD_DIM ** -0.5)
            e = jnp.exp(s - jnp.max(s, axis=-1, keepdims=True))
            inv = 1.0 / jnp.sum(e, axis=-1, keepdims=True)
            outs.append(_dot(e.astype(BF16), mv_ref[0, 0, h]) * inv)
            fill(1)
        parts.append(_merge_pair(outs[0], outs[1]))
    out_c = jnp.concatenate(parts, axis=-1)

    merged = None
    for n, br in enumerate((out_a, out_b, out_c)):
        term = _sigmoid(take(C_GATE + n * D_MODEL, D_MODEL)) * _dot(br.astype(BF16), wb_ref[n])
        merged = term if merged is None else merged + term
    o_ref[0] = x + _dot(merged.astype(BF16), wo_ref[...])


def _resident(shape):
    zeros = (0,) * len(shape)
    return pl.BlockSpec(shape, lambda b, j: zeros, pipeline_mode=pl.Buffered(1))


def _mixer(x, mk, mv, layer, p):
    bsz, slen, _ = x.shape
    t = min(MIX_TILE, slen)
    pad = t // 2
    tile = pl.BlockSpec((1, t, D_MODEL), lambda b, j: (b, j, 0))
    kv_spec = pl.BlockSpec((1, 1, HEADS, MEM_LEN, WIN), lambda b, j: (layer, b, 0, 0, 0))
    weights = (p["gain"], p["w1"], p["cw"], p["cb"], p["wg"], p["bg"], p["lam"], p["wdec"], p["bdec"],
               p["gn"])
    tail = (p["wb"], p["wo"])
    return pl.pallas_call(
        _mixer_kernel,
        grid=(bsz, slen // t),
        in_specs=[tile] + [_resident(w.shape) for w in weights] + [kv_spec, kv_spec]
        + [_resident(w.shape) for w in tail],
        out_specs=tile,
        out_shape=jax.ShapeDtypeStruct(x.shape, F32),
        scratch_shapes=[
            pltpu.VMEM((SUBLANES, MIX), F32),
            pltpu.VMEM((pad + t, MIX), F32),
            pltpu.VMEM((pad + t, MIX), F32),
            pltpu.VMEM((SUBLANES, MIX), F32),
            pltpu.VMEM((HEADS, WIN, DK_PAD), F32),
        ],
        compiler_params=pltpu.CompilerParams(
            dimension_semantics=("arbitrary", "arbitrary"), vmem_limit_bytes=VMEM_LIMIT),
        name="mixer",
    )(x, *weights, mk, mv, *tail)


GELU_C0 = math.sqrt(2.0 / math.pi)
GELU_C1 = GELU_C0 * 0.044715


def _ffn_kernel(x_ref, gain_ref, wup_ref, cw_ref, cb_ref, wdn_ref, fin_ref, o_ref, ut_s, *, final):
    t = x_ref.shape[1]
    j = pl.program_id(1)

    @pl.when(j == 0)
    def _():
        ut_s[...] = jnp.zeros_like(ut_s)

    x = x_ref[0]
    hb = _rms(x, gain_ref[...]).astype(BF16)

    def conv_cols(col, scale):
        cols = slice(col, col + FFN_COLS)
        up = _dot(hb, wup_ref[:, cols])
        tail = ut_s[:, cols]
        cw = cw_ref[:, cols] * scale
        y = cb_ref[:, cols] * scale + cw[FFN_CONV - 1:FFN_CONV, :] * up
        for k in range(1, FFN_CONV):
            y = y + cw[FFN_CONV - 1 - k:FFN_CONV - k, :] * _shift_rows(up, tail, k)
        ut_s[:, cols] = up[t - SUBLANES:, :]
        return y

    acc = x
    for c in range(D_FF // FFN_COLS):
        gate_h = conv_cols(c * FFN_COLS, 1.0)
        half_val = conv_cols(D_FF + c * FFN_COLS, 0.5)
        th = jnp.tanh(gate_h * (GELU_C0 + GELU_C1 * (gate_h * gate_h)))
        act = ((gate_h + gate_h * th) * half_val).astype(BF16)
        acc = acc + _dot(act, wdn_ref[c * FFN_COLS:(c + 1) * FFN_COLS, :])
    o_ref[0] = _rms(acc, fin_ref[...]) if final else acc


def _ffn(x, p, final_gain, final):
    bsz, slen, _ = x.shape
    t = min(FFN_TILE, slen)
    tile = pl.BlockSpec((1, t, D_MODEL), lambda b, j: (b, j, 0))
    weights = (p["gain2"], p["wup"], p["cw2"], p["cb2"], p["wdn"], final_gain)
    return pl.pallas_call(
        functools.partial(_ffn_kernel, final=final),
        grid=(bsz, slen // t),
        in_specs=[tile] + [_resident(w.shape) for w in weights],
        out_specs=tile,
        out_shape=jax.ShapeDtypeStruct(x.shape, F32),
        scratch_shapes=[pltpu.VMEM((SUBLANES, 2 * D_FF), F32)],
        compiler_params=pltpu.CompilerParams(
            dimension_semantics=("arbitrary", "arbitrary"), vmem_limit_bytes=VMEM_LIMIT),
        name="ffn",
    )(x, *weights)


def _pad_heads(w, width, padded):
    lead = w.shape[:-1]
    w = w.reshape(*lead, HEADS, width)
    w = jnp.pad(w, [(0, 0)] * len(lead) + [(0, 0), (0, padded - width)])
    return w.reshape(*lead, HEADS * padded)


def _block_diag(w):
    n, rows, cols = w.shape
    eye = jnp.eye(n, dtype=w.dtype)
    return jnp.einsum("hij,hg->higj", w, eye).reshape(n * rows, n * cols)


def _gate_weights(w_a, w_x):
    per = RNN_BLOCKS // RNN_GROUPS
    return jnp.stack([
        jnp.concatenate([_block_diag(w_a[g * per:(g + 1) * per]),
                         _block_diag(w_x[g * per:(g + 1) * per])], axis=1)
        for g in range(RNN_GROUPS)])


def _pack_layer(l, w_in, norm_mix, rnn_conv_w, rnn_conv_b, rg_w_a, rg_b_a, rg_w_x, rg_b_x, rg_lambda,
                gla_w_decay, gla_b_decay, gla_norm, w_branch, w_out, norm_ffn, w_up, ffn_conv_w,
                ffn_conv_b, w_down):
    bounds = [0]
    for width in IN_SPLITS:
        bounds.append(bounds[-1] + width)
    w_r, w_q, w_k, w_v, w_g, w_d, w_mq, w_gate = (
        w_in[l][:, bounds[i]:bounds[i + 1]] for i in range(len(IN_SPLITS)))
    w_q = _pad_heads(w_q, GLA_DK, DK_PAD)
    w_q = w_q.at[:, GLA_DK:GLA_DK + GLA_LOW_RANK].set(w_d)
    w1 = jnp.concatenate(
        [w_r, w_q, _pad_heads(w_k, GLA_DK, DK_PAD), w_v, w_g, w_mq, w_gate], axis=1).astype(BF16)
    wdec = jnp.zeros((DK_PAD, HEADS * DK_PAD), F32).at[GLA_DK:GLA_DK + GLA_LOW_RANK, :].set(
        _pad_heads(gla_w_decay[l], GLA_DK, DK_PAD))
    b_a = rg_b_a[l].reshape(RNN_GROUPS, 1, RNN_GROUP)
    b_x = rg_b_x[l].reshape(RNN_GROUPS, 1, RNN_GROUP)
    return {
        "gain": norm_mix[l].reshape(1, D_MODEL),
        "w1": w1,
        "cw": rnn_conv_w[l],
        "cb": rnn_conv_b[l].reshape(1, MIX),
        "wg": _gate_weights(rg_w_a[l], rg_w_x[l]).astype(BF16),
        "bg": jnp.concatenate([b_a, b_x], axis=-1),
        "lam": rg_lambda[l].reshape(1, MIX),
        "wdec": wdec.astype(BF16),
        "bdec": _pad_heads(gla_b_decay[l].reshape(1, HEADS * GLA_DK), GLA_DK, DK_PAD),
        "gn": gla_norm[l].reshape(1, MIX),
        "wb": w_branch[l].astype(BF16),
        "wo": w_out[l].astype(BF16),
        "gain2": norm_ffn[l].reshape(1, D_MODEL),
        "wup": w_up[l].astype(BF16),
        "cw2": ffn_conv_w[l],
        "cb2": ffn_conv_b[l].reshape(1, 2 * D_FF),
        "wdn": w_down[l].astype(BF16),
    }


def kernel(x, mem, norm_mix, w_in, rnn_conv_w, rnn_conv_b, rg_w_a, rg_b_a, rg_w_x, rg_b_x, rg_lambda,
           gla_w_decay, gla_b_decay, gla_norm, mem_norm, w_mem_kv, w_branch, w_out, norm_ffn, w_up,
           ffn_conv_w, ffn_conv_b, w_down, final_norm):
    depth = w_in.shape[0]
    assert x.shape[1] % CHUNK == 0 and x.shape[2] == D_MODEL and mem.shape[1:] == (MEM_LEN, D_MODEL)
    mk, mv = _memkv(mem, mem_norm, w_mem_kv)
    final_gain = final_norm.reshape(1, D_MODEL)
    for l in range(depth):
        p = _pack_layer(l, w_in, norm_mix, rnn_conv_w, rnn_conv_b, rg_w_a, rg_b_a, rg_w_x, rg_b_x,
                        rg_lambda, gla_w_decay, gla_b_decay, gla_norm, w_branch, w_out, norm_ffn,
                        w_up, ffn_conv_w, ffn_conv_b, w_down)
        x = _mixer(x, mk, mv, l, p)
        x = _ffn(x, p, final_gain, final=(l == depth - 1))
    return x
```

```python
import functools
import math

import jax
import jax.numpy as jnp
from jax import lax
from jax.experimental import pallas as pl
from jax.experimental.pallas import tpu as pltpu

D_MODEL = 1024
MEM_LEN = 256
MIX = 768
N_BRANCHES = 3
RNN_BLOCKS = 8
RNN_BLOCK = MIX // RNN_BLOCKS
RNN_GROUPS = 2
RNN_GROUP = MIX // RNN_GROUPS
RNN_CONV = 4
LRU_C = 8.0
HEADS = 4
GLA_DK = 96
GLA_DV = 192
GLA_LOW_RANK = 16
GLA_NORMALIZER = 16.0
CHUNK = 64
CHUNK_SHIFT = CHUNK.bit_length() - 1
MEM_HEAD_DIM = 192
D_FF = 3 * D_MODEL
FFN_CONV = 3
EPS = 1e-6
IN_SPLITS = (MIX, HEADS * GLA_DK, HEADS * GLA_DK, HEADS * GLA_DV, HEADS * GLA_DV, GLA_LOW_RANK,
             HEADS * MEM_HEAD_DIM, N_BRANCHES * D_MODEL)

LANES = 128
SUBLANES = 8
DK_PAD = LANES
WIN = 2 * LANES
PAIR = 3 * LANES

C_RNN = 0
C_Q = C_RNN + MIX
C_K = C_Q + HEADS * DK_PAD
C_V = C_K + HEADS * DK_PAD
C_G = C_V + MIX
C_MQ = C_G + MIX
C_GATE = C_MQ + MIX
N_IN_PACKED = C_GATE + N_BRANCHES * D_MODEL

MIX_TILE = 512
SCAN_ROWS = 128
FFN_TILE = 512
FFN_COLS = 3072
VMEM_LIMIT = 56 * 1024 * 1024

BF16 = jnp.bfloat16
F32 = jnp.float32


def _dot(a, b):
    return jnp.dot(a, b, preferred_element_type=F32)


def _dot_nt(a, b):
    return lax.dot_general(a, b, (((1,), (1,)), ((), ())), preferred_element_type=F32)


def _dot_tn(a, b):
    return lax.dot_general(a, b, (((0,), (0,)), ((), ())), preferred_element_type=F32)


def _rms(x, gain):
    return x * lax.rsqrt(jnp.mean(x * x, axis=-1, keepdims=True) + EPS) * gain


NEG_LOG2_E = -1.4426950408889634


def _sigmoid(x):
    return 1.0 / (1.0 + jnp.exp2(x * NEG_LOG2_E))


def _log_sigmoid(x):
    return jnp.minimum(x, 0.0) - jnp.log1p(jnp.exp(-jnp.abs(x)))


def _log_sigmoid_wide(x):
    return jnp.minimum(x, 0.0) - jnp.log(1.0 + jnp.exp(-jnp.abs(x)))


def _sqrt_nonneg(y):
    return jnp.where(y > 0.0, y * lax.rsqrt(y), 0.0)


def _shift_rows(x, tail, k):
    rolled = pltpu.roll(x, k, 0)
    row = lax.broadcasted_iota(jnp.int32, (SUBLANES, x.shape[1]), 0)
    top = jnp.where(row < k, pltpu.roll(tail, k, 0), rolled[:SUBLANES])
    return jnp.concatenate([top, rolled[SUBLANES:]], axis=0)


def _head_window(h):
    pair, odd = divmod(h, 2)
    start = pair * PAIR + odd * LANES
    lo = odd * (GLA_DV - LANES)
    return start, lo, lo + GLA_DV


def _valid_lanes(h, rows):
    _, lo, hi = _head_window(h)
    lane = lax.broadcasted_iota(jnp.int32, (rows, WIN), 1)
    return (lane >= lo) & (lane < hi)


def _merge_pair(even, odd):
    return jnp.concatenate(
        [even[:, :LANES], even[:, LANES:] + odd[:, :LANES], odd[:, LANES:]], axis=-1)


def _memkv_kernel(mem_ref, gain_ref, w_ref, mk_ref, mv_ref):
    m = _rms(mem_ref[0], gain_ref[0]).astype(BF16)
    kv = _dot(m, w_ref[0])
    for h in range(HEADS):
        start, _, _ = _head_window(h)
        valid = _valid_lanes(h, MEM_LEN)
        mk_ref[0, 0, h] = jnp.where(valid, kv[:, start:start + WIN], 0.0).astype(BF16)
        mv_ref[0, 0, h] = jnp.where(valid, kv[:, MIX + start:MIX + start + WIN], 0.0).astype(BF16)


def _memkv(mem, mem_norm, w_mem_kv):
    depth = w_mem_kv.shape[0]
    bsz = mem.shape[0]
    out = jax.ShapeDtypeStruct((depth, bsz, HEADS, MEM_LEN, WIN), BF16)
    out_spec = pl.BlockSpec((1, 1, HEADS, MEM_LEN, WIN), lambda l, b: (l, b, 0, 0, 0))
    return pl.pallas_call(
        _memkv_kernel,
        grid=(depth, bsz),
        in_specs=[
            pl.BlockSpec((1, MEM_LEN, D_MODEL), lambda l, b: (b, 0, 0)),
            pl.BlockSpec((1, 1, D_MODEL), lambda l, b: (l, 0, 0)),
            pl.BlockSpec((1, D_MODEL, 2 * MIX), lambda l, b: (l, 0, 0)),
        ],
        out_specs=[out_spec, out_spec],
        out_shape=[out, out],
        compiler_params=pltpu.CompilerParams(
            dimension_semantics=("arbitrary", "arbitrary"), vmem_limit_bytes=VMEM_LIMIT),
        name="memkv",
    )(mem, mem_norm.reshape(depth, 1, D_MODEL), w_mem_kv.astype(BF16))


def _mixer_kernel(x_ref, gain_ref, w1_ref, cw_ref, cb_ref, wg_ref, bg_ref, lam_ref, wdec_ref,
                  bdec_ref, gn_ref, mk_ref, mv_ref, wb_ref, wo_ref, o_ref,
                  zt_s, hp_s, st_s):
    t = x_ref.shape[1]
    j = pl.program_id(1)

    @pl.when(j == 0)
    def _():
        zt_s[...] = jnp.zeros_like(zt_s)
        hp_s[...] = jnp.zeros_like(hp_s)
        st_s[...] = jnp.zeros_like(st_s)

    x = x_ref[0]
    hb = _rms(x, gain_ref[...]).astype(BF16)

    def proj(col, width):
        return _dot(hb, w1_ref[:, col:col + width])

    queue = [col + i * WIN for col, width in ((C_Q, HEADS * DK_PAD), (C_K, HEADS * DK_PAD), (C_V, MIX),
                                              (C_MQ, MIX), (C_G, MIX), (C_GATE, N_BRANCHES * D_MODEL))
             for i in range(width // WIN)]
    ready = {}

    def fill(n):
        for _ in range(min(n, len(queue))):
            col = queue.pop(0)
            ready[col] = proj(col, WIN)

    def take(col, width):
        cols = [col + i * WIN for i in range(width // WIN)]
        while any(c not in ready for c in cols):
            fill(1)
        return jnp.concatenate([ready.pop(c) for c in cols], axis=-1)

    rb = min(SCAN_ROWS, t)
    blocks = [slice(i * rb, (i + 1) * rb) for i in range(t // rb)]
    z_r = proj(C_RNN, MIX)
    xr_parts = []
    tail = zt_s[...]
    for i, rows_b in enumerate(blocks):
        z_b = z_r[rows_b]
        xr_b = cb_ref[...] + cw_ref[RNN_CONV - 1:RNN_CONV, :] * z_b
        for k in range(1, RNN_CONV):
            xr_b = xr_b + cw_ref[RNN_CONV - 1 - k:RNN_CONV - k, :] * _shift_rows(z_b, tail, k)
        xr_parts.append(xr_b)
        tail = z_b[rb - SUBLANES:]
        fill(i % 2)
    zt_s[...] = tail
    xr = jnp.concatenate(xr_parts, axis=0)

    xr_bf = xr.astype(BF16)
    r_parts, i_parts = [], []
    for grp in range(RNN_GROUPS):
        lanes = slice(grp * RNN_GROUP, (grp + 1) * RNN_GROUP)
        gates = _dot(xr_bf[:, lanes], wg_ref[grp]) + bg_ref[grp]
        r_parts.append(gates[:, :RNN_GROUP])
        i_parts.append(gates[:, RNN_GROUP:])
    r_pre = jnp.concatenate(r_parts, axis=-1)
    i_pre = jnp.concatenate(i_parts, axis=-1)
    decay_rate = LRU_C * _log_sigmoid(lam_ref[...])
    row8 = lax.broadcasted_iota(jnp.int32, (SUBLANES, MIX), 0)
    zeros8 = jnp.zeros((SUBLANES, MIX), F32)
    ones8 = jnp.ones((SUBLANES, MIX), F32)
    h_prev = hp_s[0:1, :]
    out_parts = []
    for rows_b, xr_b in zip(blocks, xr_parts):
        a = jnp.exp(decay_rate * _sigmoid(r_pre[rows_b]))
        u = _sqrt_nonneg(1.0 - a * a) * (_sigmoid(i_pre[rows_b]) * xr_b)
        fill(1)
        u = jnp.concatenate(
            [u[:SUBLANES] + jnp.where(row8 == 0, a[:SUBLANES] * h_prev, 0.0), u[SUBLANES:]], axis=0)
        k = 1
        while k < rb:
            if k < SUBLANES:
                u_prev = _shift_rows(u, zeros8, k)
                a_prev = _shift_rows(a, ones8, k)
            else:
                u_prev = jnp.concatenate([jnp.zeros((k, MIX), F32), u[:rb - k]], axis=0)
                a_prev = jnp.concatenate([jnp.ones((k, MIX), F32), a[:rb - k]], axis=0)
            u = u + a * u_prev
            if 2 * k < rb:
                a = a * a_prev
            if k == SUBLANES // 2:
                fill(1)
            k *= 2
        out_parts.append(u)
        h_prev = u[rb - 1:rb, :]
    out_a = jnp.concatenate(out_parts, axis=0)
    hp_s[0:1, :] = h_prev

    q = take(C_Q, HEADS * DK_PAD)
    kk = take(C_K, HEADS * DK_PAD)
    v = take(C_V, MIX).astype(BF16)
    logit = _dot(q[:, :DK_PAD].astype(BF16), wdec_ref[...]) + bdec_ref[...]
    g = _log_sigmoid_wide(logit) * (1.0 / GLA_NORMALIZER)
    g_hi = g.astype(BF16)
    g_lo = (g - g_hi.astype(F32)).astype(BF16)
    ri = lax.broadcasted_iota(jnp.int32, (t, t), 0)
    ci = lax.broadcasted_iota(jnp.int32, (t, t), 1)
    same_chunk = lax.shift_right_logical(ri, CHUNK_SHIFT) == lax.shift_right_logical(ci, CHUNK_SHIFT)
    m_cum = jnp.where(same_chunk & (ci <= ri), 1.0, 0.0).astype(BF16)
    g_cum = _dot(m_cum, g_hi) + _dot(m_cum, g_lo)
    n_chunks = t // CHUNK
    g_tot = jnp.concatenate(
        [jnp.broadcast_to(g_cum[(c + 1) * CHUNK - 1:(c + 1) * CHUNK, :], (CHUNK, HEADS * DK_PAD))
         for c in range(n_chunks)], axis=0)
    qe = (q * ((GLA_DK ** -0.5) * jnp.exp(g_cum))).astype(BF16)
    fill(1)
    ke = (kk * jnp.exp(-g_cum)).astype(BF16)
    fill(1)
    kd = (kk * jnp.exp(g_tot - g_cum)).astype(BF16)
    fill(1)
    decay = jnp.exp(g_tot)

    cr = lax.broadcasted_iota(jnp.int32, (CHUNK, CHUNK), 0)
    cc = lax.broadcasted_iota(jnp.int32, (CHUNK, CHUNK), 1)
    causal = cc <= cr

    def rows(c):
        return slice(c * CHUNK, (c + 1) * CHUNK)

    def head_cols(h):
        return slice(h * DK_PAD, (h + 1) * DK_PAD)

    def window(h):
        start, _, _ = _head_window(h)
        return slice(start, start + WIN)

    scores = [[jnp.where(causal, _dot_nt(qe[rows(c), head_cols(h)], ke[rows(c), head_cols(h)]), 0.0)
               .astype(BF16) for h in range(HEADS)] for c in range(n_chunks)]
    kv_new = [[_dot_tn(v[rows(c), window(h)], kd[rows(c), head_cols(h)]) for h in range(HEADS)]
              for c in range(n_chunks)]
    state_in = [[None] * HEADS for _ in range(n_chunks)]
    for h in range(HEADS):
        state = st_s[h]
        for c in range(n_chunks):
            state_in[c][h] = state.astype(BF16)
            state = state * decay[c * CHUNK:c * CHUNK + 1, head_cols(h)] + kv_new[c][h]
        st_s[h] = state
    chunk_out = []
    for c in range(n_chunks):
        pairs = []
        for p in range(HEADS // 2):
            outs = []
            for h in (2 * p, 2 * p + 1):
                valid = _valid_lanes(h, CHUNK)
                o = (_dot(scores[c][h], v[rows(c), window(h)])
                     + _dot_nt(qe[rows(c), head_cols(h)], state_in[c][h]))
                ms = jnp.sum(jnp.where(valid, o * o, 0.0), axis=-1, keepdims=True) * (1.0 / GLA_DV)
                y = o * lax.rsqrt(ms + EPS) * gn_ref[:, window(h)]
                outs.append(jnp.where(valid, y, 0.0))
            pairs.append(_merge_pair(outs[0], outs[1]))
        chunk_out.append(jnp.concatenate(pairs, axis=-1))
        fill(1)
    z_g = take(C_G, MIX)
    out_b = jnp.concatenate(chunk_out, axis=0) * (z_g * _sigmoid(z_g))

    mq = take(C_MQ, MIX).astype(BF16)
    parts = []
    for p in range(HEADS // 2):
        outs = []
        for h in (2 * p, 2 * p + 1):
            s = _dot_nt(mq[:, window(h)], mk_ref[0, 0, h]) * (MEM_HEAD_DIM ** -0.5)
            e = jnp.exp(s - jnp.max(s, axis=-1, keepdims=True))
            inv = 1.0 / jnp.sum(e, axis=-1, keepdims=True)
            outs.append(_dot(e.astype(BF16), mv_ref[0, 0, h]) * inv)
            fill(1)
        parts.append(_merge_pair(outs[0], outs[1]))
    out_c = jnp.concatenate(parts, axis=-1)

    merged = None
    for n, br in enumerate((out_a, out_b, out_c)):
        term = _sigmoid(take(C_GATE + n * D_MODEL, D_MODEL)) * _dot(br.astype(BF16), wb_ref[n])
        merged = term if merged is None else merged + term
    o_ref[0] = x + _dot(merged.astype(BF16), wo_ref[...])


def _resident(stacked, layer):
    zeros = (0,) * (stacked.ndim - 1)
    return pl.BlockSpec((None,) + stacked.shape[1:], lambda b, j: (layer,) + zeros,
                        pipeline_mode=pl.Buffered(1))


def _mixer(x, mk, mv, layer, p):
    bsz, slen, _ = x.shape
    t = min(MIX_TILE, slen)
    tile = pl.BlockSpec((1, t, D_MODEL), lambda b, j: (b, j, 0))
    kv_spec = pl.BlockSpec((1, 1, HEADS, MEM_LEN, WIN), lambda b, j: (layer, b, 0, 0, 0))
    weights = (p["gain"], p["w1"], p["cw"], p["cb"], p["wg"], p["bg"], p["lam"], p["wdec"], p["bdec"],
               p["gn"])
    tail = (p["wb"], p["wo"])
    return pl.pallas_call(
        _mixer_kernel,
        grid=(bsz, slen // t),
        in_specs=[tile] + [_resident(w, layer) for w in weights] + [kv_spec, kv_spec]
        + [_resident(w, layer) for w in tail],
        out_specs=tile,
        out_shape=jax.ShapeDtypeStruct(x.shape, F32),
        scratch_shapes=[
            pltpu.VMEM((SUBLANES, MIX), F32),
            pltpu.VMEM((SUBLANES, MIX), F32),
            pltpu.VMEM((HEADS, WIN, DK_PAD), F32),
        ],
        compiler_params=pltpu.CompilerParams(
            dimension_semantics=("arbitrary", "arbitrary"), vmem_limit_bytes=VMEM_LIMIT),
        name="mixer",
    )(x, *weights, mk, mv, *tail)


GELU_C0 = math.sqrt(2.0 / math.pi)
GELU_C1 = GELU_C0 * 0.044715


def _ffn_kernel(x_ref, gain_ref, wup_ref, cw_ref, cb_ref, wdn_ref, fin_ref, o_ref, ut_s, *, final):
    t = x_ref.shape[1]
    j = pl.program_id(1)

    @pl.when(j == 0)
    def _():
        ut_s[...] = jnp.zeros_like(ut_s)

    x = x_ref[0]
    hb = _rms(x, gain_ref[...]).astype(BF16)

    def conv_cols(col, scale):
        cols = slice(col, col + FFN_COLS)
        up = _dot(hb, wup_ref[:, cols])
        tail = ut_s[:, cols]
        cw = cw_ref[:, cols] * scale
        y = cb_ref[:, cols] * scale + cw[FFN_CONV - 1:FFN_CONV, :] * up
        for k in range(1, FFN_CONV):
            y = y + cw[FFN_CONV - 1 - k:FFN_CONV - k, :] * _shift_rows(up, tail, k)
        ut_s[:, cols] = up[t - SUBLANES:, :]
        return y

    acc = x
    for c in range(D_FF // FFN_COLS):
        gate_h = conv_cols(c * FFN_COLS, 1.0)
        half_val = conv_cols(D_FF + c * FFN_COLS, 0.5)
        th = jnp.tanh(gate_h * (GELU_C0 + GELU_C1 * (gate_h * gate_h)))
        act = ((gate_h + gate_h * th) * half_val).astype(BF16)
        acc = acc + _dot(act, wdn_ref[c * FFN_COLS:(c + 1) * FFN_COLS, :])
    o_ref[0] = _rms(acc, fin_ref[...]) if final else acc


def _ffn(x, layer, p, final_gain, final):
    bsz, slen, _ = x.shape
    t = min(FFN_TILE, slen)
    tile = pl.BlockSpec((1, t, D_MODEL), lambda b, j: (b, j, 0))
    weights = (p["gain2"], p["wup"], p["cw2"], p["cb2"], p["wdn"])
    return pl.pallas_call(
        functools.partial(_ffn_kernel, final=final),
        grid=(bsz, slen // t),
        in_specs=[tile] + [_resident(w, layer) for w in weights] + [_resident(final_gain, 0)],
        out_specs=tile,
        out_shape=jax.ShapeDtypeStruct(x.shape, F32),
        scratch_shapes=[pltpu.VMEM((SUBLANES, 2 * D_FF), F32)],
        compiler_params=pltpu.CompilerParams(
            dimension_semantics=("arbitrary", "arbitrary"), vmem_limit_bytes=VMEM_LIMIT),
        name="ffn",
    )(x, *weights, final_gain)


def _pad_heads(w, width, padded):
    lead = w.shape[:-1]
    w = w.reshape(*lead, HEADS, width)
    w = jnp.pad(w, [(0, 0)] * len(lead) + [(0, 0), (0, padded - width)])
    return w.reshape(*lead, HEADS * padded)


def _block_diag(w):
    n, rows, cols = w.shape
    eye = jnp.eye(n, dtype=w.dtype)
    return jnp.einsum("hij,hg->higj", w, eye).reshape(n * rows, n * cols)


def _gate_weights(w_a, w_x):
    per = RNN_BLOCKS // RNN_GROUPS
    return jnp.stack([
        jnp.concatenate([_block_diag(w_a[g * per:(g + 1) * per]),
                         _block_diag(w_x[g * per:(g + 1) * per])], axis=1)
        for g in range(RNN_GROUPS)])


def _pack_weights(w_in, norm_mix, rnn_conv_w, rnn_conv_b, rg_w_a, rg_b_a, rg_w_x, rg_b_x, rg_lambda,
                  gla_w_decay, gla_b_decay, gla_norm, w_branch, w_out, norm_ffn, w_up, ffn_conv_w,
                  ffn_conv_b, w_down):
    depth = w_in.shape[0]
    bounds = [0]
    for width in IN_SPLITS:
        bounds.append(bounds[-1] + width)
    w_r, w_q, w_k, w_v, w_g, w_d, w_mq, w_gate = (
        w_in[:, :, bounds[i]:bounds[i + 1]] for i in range(len(IN_SPLITS)))
    w_q = _pad_heads(w_q, GLA_DK, DK_PAD)
    w_q = w_q.at[:, :, GLA_DK:GLA_DK + GLA_LOW_RANK].set(w_d)
    w1 = jnp.concatenate(
        [w_r, w_q, _pad_heads(w_k, GLA_DK, DK_PAD), w_v, w_g, w_mq, w_gate], axis=2).astype(BF16)
    wdec = jnp.zeros((depth, DK_PAD, HEADS * DK_PAD), F32).at[:, GLA_DK:GLA_DK + GLA_LOW_RANK, :].set(
        _pad_heads(gla_w_decay, GLA_DK, DK_PAD))
    b_a = rg_b_a.reshape(depth, RNN_GROUPS, 1, RNN_GROUP)
    b_x = rg_b_x.reshape(depth, RNN_GROUPS, 1, RNN_GROUP)
    return {
        "gain": norm_mix.reshape(depth, 1, D_MODEL),
        "w1": w1,
        "cw": rnn_conv_w,
        "cb": rnn_conv_b.reshape(depth, 1, MIX),
        "wg": jax.vmap(_gate_weights)(rg_w_a, rg_w_x).astype(BF16),
        "bg": jnp.concatenate([b_a, b_x], axis=-1),
        "lam": rg_lambda.reshape(depth, 1, MIX),
        "wdec": wdec.astype(BF16),
        "bdec": _pad_heads(gla_b_decay.reshape(depth, 1, HEADS * GLA_DK), GLA_DK, DK_PAD),
        "gn": gla_norm.reshape(depth, 1, MIX),
        "wb": w_branch.astype(BF16),
        "wo": w_out.astype(BF16),
        "gain2": norm_ffn.reshape(depth, 1, D_MODEL),
        "wup": w_up.astype(BF16),
        "cw2": ffn_conv_w,
        "cb2": ffn_conv_b.reshape(depth, 1, 2 * D_FF),
        "wdn": w_down.astype(BF16),
    }


def kernel(x, mem, norm_mix, w_in, rnn_conv_w, rnn_conv_b, rg_w_a, rg_b_a, rg_w_x, rg_b_x, rg_lambda,
           gla_w_decay, gla_b_decay, gla_norm, mem_norm, w_mem_kv, w_branch, w_out, norm_ffn, w_up,
           ffn_conv_w, ffn_conv_b, w_down, final_norm):
    depth = w_in.shape[0]
    assert x.shape[1] % CHUNK == 0 and x.shape[2] == D_MODEL and mem.shape[1:] == (MEM_LEN, D_MODEL)
    mk, mv = _memkv(mem, mem_norm, w_mem_kv)
    p = _pack_weights(w_in, norm_mix, rnn_conv_w, rnn_conv_b, rg_w_a, rg_b_a, rg_w_x, rg_b_x, rg_lambda,
                      gla_w_decay, gla_b_decay, gla_norm, w_branch, w_out, norm_ffn, w_up, ffn_conv_w,
                      ffn_conv_b, w_down)
    final_gain = final_norm.reshape(1, 1, D_MODEL)
    for l in range(depth):
        x = _mixer(x, mk, mv, l, p)
        x = _ffn(x, l, p, final_gain, final=(l == depth - 1))
    return x
```

```python
import functools
import math

import jax
import jax.numpy as jnp
from jax import lax
from jax.experimental import pallas as pl
from jax.experimental.pallas import tpu as pltpu

D_MODEL = 1024
MEM_LEN = 256
MIX = 768
N_BRANCHES = 3
RNN_BLOCKS = 8
RNN_BLOCK = MIX // RNN_BLOCKS
RNN_GROUPS = 2
RNN_GROUP = MIX // RNN_GROUPS
RNN_CONV = 4
LRU_C = 8.0
HEADS = 4
GLA_DK = 96
GLA_DV = 192
GLA_LOW_RANK = 16
GLA_NORMALIZER = 16.0
CHUNK = 64
CHUNK_SHIFT = CHUNK.bit_length() - 1
MEM_HEAD_DIM = 192
D_FF = 3 * D_MODEL
FFN_CONV = 3
EPS = 1e-6
IN_SPLITS = (MIX, HEADS * GLA_DK, HEADS * GLA_DK, HEADS * GLA_DV, HEADS * GLA_DV, GLA_LOW_RANK,
             HEADS * MEM_HEAD_DIM, N_BRANCHES * D_MODEL)

LANES = 128
SUBLANES = 8
DK_PAD = LANES
WIN = 2 * LANES
PAIR = 3 * LANES

C_RNN = 0
C_Q = C_RNN + MIX
C_K = C_Q + HEADS * DK_PAD
C_V = C_K + HEADS * DK_PAD
C_G = C_V + MIX
C_MQ = C_G + MIX
C_GATE = C_MQ + MIX
N_IN_PACKED = C_GATE + N_BRANCHES * D_MODEL

MIX_TILE = 512
SCAN_ROWS = 128
FFN_TILE = 512
FFN_COLS = 3072
VMEM_LIMIT = 56 * 1024 * 1024

BF16 = jnp.bfloat16
F32 = jnp.float32


def _dot(a, b):
    return jnp.dot(a, b, preferred_element_type=F32)


def _dot_nt(a, b):
    return lax.dot_general(a, b, (((1,), (1,)), ((), ())), preferred_element_type=F32)


def _dot_tn(a, b):
    return lax.dot_general(a, b, (((0,), (0,)), ((), ())), preferred_element_type=F32)


def _rms(x, gain):
    return x * lax.rsqrt(jnp.mean(x * x, axis=-1, keepdims=True) + EPS) * gain


NEG_LOG2_E = -1.4426950408889634


def _sigmoid(x):
    return 1.0 / (1.0 + jnp.exp2(x * NEG_LOG2_E))


def _log_sigmoid(x):
    return jnp.minimum(x, 0.0) - jnp.log1p(jnp.exp(-jnp.abs(x)))


def _log_sigmoid_wide(x):
    return jnp.minimum(x, 0.0) - jnp.log(1.0 + jnp.exp(-jnp.abs(x)))


def _sqrt_nonneg(y):
    return jnp.where(y > 0.0, y * lax.rsqrt(y), 0.0)


def _shift_rows(x, tail, k):
    rolled = pltpu.roll(x, k, 0)
    row = lax.broadcasted_iota(jnp.int32, (SUBLANES, x.shape[1]), 0)
    top = jnp.where(row < k, pltpu.roll(tail, k, 0), rolled[:SUBLANES])
    return jnp.concatenate([top, rolled[SUBLANES:]], axis=0)


def _head_window(h):
    pair, odd = divmod(h, 2)
    start = pair * PAIR + odd * LANES
    lo = odd * (GLA_DV - LANES)
    return start, lo, lo + GLA_DV


def _valid_lanes(h, rows):
    _, lo, hi = _head_window(h)
    lane = lax.broadcasted_iota(jnp.int32, (rows, WIN), 1)
    return (lane >= lo) & (lane < hi)


def _merge_pair(even, odd):
    return jnp.concatenate(
        [even[:, :LANES], even[:, LANES:] + odd[:, :LANES], odd[:, LANES:]], axis=-1)


def _memkv_kernel(mem_ref, gain_ref, w_ref, mk_ref, mv_ref):
    m = _rms(mem_ref[0], gain_ref[0]).astype(BF16)
    kv = _dot(m, w_ref[0])
    for h in range(HEADS):
        start, _, _ = _head_window(h)
        valid = _valid_lanes(h, MEM_LEN)
        mk_ref[0, 0, h] = jnp.where(valid, kv[:, start:start + WIN], 0.0).astype(BF16)
        mv_ref[0, 0, h] = jnp.where(valid, kv[:, MIX + start:MIX + start + WIN], 0.0).astype(BF16)


def _memkv(mem, mem_norm, w_mem_kv):
    depth = w_mem_kv.shape[0]
    bsz = mem.shape[0]
    out = jax.ShapeDtypeStruct((depth, bsz, HEADS, MEM_LEN, WIN), BF16)
    out_spec = pl.BlockSpec((1, 1, HEADS, MEM_LEN, WIN), lambda l, b: (l, b, 0, 0, 0))
    return pl.pallas_call(
        _memkv_kernel,
        grid=(depth, bsz),
        in_specs=[
            pl.BlockSpec((1, MEM_LEN, D_MODEL), lambda l, b: (b, 0, 0)),
            pl.BlockSpec((1, 1, D_MODEL), lambda l, b: (l, 0, 0)),
            pl.BlockSpec((1, D_MODEL, 2 * MIX), lambda l, b: (l, 0, 0)),
        ],
        out_specs=[out_spec, out_spec],
        out_shape=[out, out],
        compiler_params=pltpu.CompilerParams(
            dimension_semantics=("arbitrary", "arbitrary"), vmem_limit_bytes=VMEM_LIMIT),
        name="memkv",
    )(mem, mem_norm.reshape(depth, 1, D_MODEL), w_mem_kv.astype(BF16))


def _mixer_kernel(x_ref, gain_ref, w1_ref, cw_ref, cb_ref, wg_ref, bg_ref, lam_ref, wdec_ref,
                  bdec_ref, gn_ref, mk_ref, mv_ref, wb_ref, wo_ref, o_ref,
                  zt_s, hp_s, st_s):
    t = x_ref.shape[1]
    j = pl.program_id(1)

    @pl.when(j == 0)
    def _():
        zt_s[...] = jnp.zeros_like(zt_s)
        hp_s[...] = jnp.zeros_like(hp_s)
        st_s[...] = jnp.zeros_like(st_s)

    x = x_ref[0]
    hb = _rms(x, gain_ref[...]).astype(BF16)

    def proj(col, width):
        return _dot(hb, w1_ref[:, col:col + width])

    queue = [col + i * WIN for col, width in ((C_Q, HEADS * DK_PAD), (C_K, HEADS * DK_PAD), (C_V, MIX),
                                              (C_MQ, MIX), (C_G, MIX), (C_GATE, N_BRANCHES * D_MODEL))
             for i in range(width // WIN)]
    ready = {}

    def fill(n):
        for _ in range(min(n, len(queue))):
            col = queue.pop(0)
            ready[col] = proj(col, WIN)

    def take(col, width):
        cols = [col + i * WIN for i in range(width // WIN)]
        while any(c not in ready for c in cols):
            fill(1)
        return jnp.concatenate([ready.pop(c) for c in cols], axis=-1)

    rb = min(SCAN_ROWS, t)
    blocks = [slice(i * rb, (i + 1) * rb) for i in range(t // rb)]
    z_r = proj(C_RNN, MIX)
    xr_parts = []
    tail = zt_s[...]
    for i, rows_b in enumerate(blocks):
        z_b = z_r[rows_b]
        xr_b = cb_ref[...] + cw_ref[RNN_CONV - 1:RNN_CONV, :] * z_b
        for k in range(1, RNN_CONV):
            xr_b = xr_b + cw_ref[RNN_CONV - 1 - k:RNN_CONV - k, :] * _shift_rows(z_b, tail, k)
        xr_parts.append(xr_b)
        tail = z_b[rb - SUBLANES:]
        fill(i % 2)
    zt_s[...] = tail
    xr = jnp.concatenate(xr_parts, axis=0)

    xr_bf = xr.astype(BF16)
    r_parts, i_parts = [], []
    for grp in range(RNN_GROUPS):
        lanes = slice(grp * RNN_GROUP, (grp + 1) * RNN_GROUP)
        gates = _dot(xr_bf[:, lanes], wg_ref[grp]) + bg_ref[grp]
        r_parts.append(gates[:, :RNN_GROUP])
        i_parts.append(gates[:, RNN_GROUP:])
    r_pre = jnp.concatenate(r_parts, axis=-1)
    i_pre = jnp.concatenate(i_parts, axis=-1)
    decay_rate = LRU_C * _log_sigmoid(lam_ref[...])
    row8 = lax.broadcasted_iota(jnp.int32, (SUBLANES, MIX), 0)
    zeros8 = jnp.zeros((SUBLANES, MIX), F32)
    ones8 = jnp.ones((SUBLANES, MIX), F32)
    h_prev = hp_s[0:1, :]
    out_parts = []
    for rows_b, xr_b in zip(blocks, xr_parts):
        a = jnp.exp(decay_rate * _sigmoid(r_pre[rows_b]))
        u = _sqrt_nonneg(1.0 - a * a) * (_sigmoid(i_pre[rows_b]) * xr_b)
        fill(1)
        u = jnp.concatenate(
            [u[:SUBLANES] + jnp.where(row8 == 0, a[:SUBLANES] * h_prev, 0.0), u[SUBLANES:]], axis=0)
        k = 1
        while k < rb:
            if k < SUBLANES:
                u_prev = _shift_rows(u, zeros8, k)
                a_prev = _shift_rows(a, ones8, k)
            else:
                u_prev = jnp.concatenate([jnp.zeros((k, MIX), F32), u[:rb - k]], axis=0)
                a_prev = jnp.concatenate([jnp.ones((k, MIX), F32), a[:rb - k]], axis=0)
            u = u + a * u_prev
            if 2 * k < rb:
                a = a * a_prev
            if k == SUBLANES // 2:
                fill(1)
            k *= 2
        out_parts.append(u)
        h_prev = u[rb - 1:rb, :]
    out_a = jnp.concatenate(out_parts, axis=0)
    hp_s[0:1, :] = h_prev

    q = take(C_Q, HEADS * DK_PAD)
    kk = take(C_K, HEADS * DK_PAD)
    v = take(C_V, MIX).astype(BF16)
    logit = _dot(q[:, :DK_PAD].astype(BF16), wdec_ref[...]) + bdec_ref[...]
    g = _log_sigmoid_wide(logit) * (1.0 / GLA_NORMALIZER)
    g_hi = g.astype(BF16)
    g_lo = (g - g_hi.astype(F32)).astype(BF16)
    cb = min(WIN, t)
    ri = lax.broadcasted_iota(jnp.int32, (cb, cb), 0)
    ci = lax.broadcasted_iota(jnp.int32, (cb, cb), 1)
    same_chunk = lax.shift_right_logical(ri, CHUNK_SHIFT) == lax.shift_right_logical(ci, CHUNK_SHIFT)
    m_cum = jnp.where(same_chunk & (ci <= ri), 1.0, 0.0).astype(BF16)
    g_cum = jnp.concatenate(
        [_dot(m_cum, g_hi[i * cb:(i + 1) * cb]) + _dot(m_cum, g_lo[i * cb:(i + 1) * cb])
         for i in range(t // cb)], axis=0)
    n_chunks = t // CHUNK
    g_tot = jnp.concatenate(
        [jnp.broadcast_to(g_cum[(c + 1) * CHUNK - 1:(c + 1) * CHUNK, :], (CHUNK, HEADS * DK_PAD))
         for c in range(n_chunks)], axis=0)
    qe = (q * ((GLA_DK ** -0.5) * jnp.exp(g_cum))).astype(BF16)
    fill(1)
    ke = (kk * jnp.exp(-g_cum)).astype(BF16)
    fill(1)
    kd = (kk * jnp.exp(g_tot - g_cum)).astype(BF16)
    fill(1)
    sel = (lax.shift_right_logical(lax.broadcasted_iota(jnp.int32, (t, LANES), 0), CHUNK_SHIFT)
           == lax.broadcasted_iota(jnp.int32, (t, LANES), 1))
    sel = jnp.where(sel, 1.0, 0.0).astype(BF16)
    decay_cols = jnp.exp(_dot_tn(g_hi, sel) + _dot_tn(g_lo, sel))

    cr = lax.broadcasted_iota(jnp.int32, (CHUNK, CHUNK), 0)
    cc = lax.broadcasted_iota(jnp.int32, (CHUNK, CHUNK), 1)
    causal = cc <= cr

    def rows(c):
        return slice(c * CHUNK, (c + 1) * CHUNK)

    def head_cols(h):
        return slice(h * DK_PAD, (h + 1) * DK_PAD)

    def window(h):
        start, _, _ = _head_window(h)
        return slice(start, start + WIN)

    scores = [[jnp.where(causal, _dot_nt(qe[rows(c), head_cols(h)], ke[rows(c), head_cols(h)]), 0.0)
               .astype(BF16) for h in range(HEADS)] for c in range(n_chunks)]
    kv_new = [[_dot_tn(kd[rows(c), head_cols(h)], v[rows(c), window(h)]) for h in range(HEADS)]
              for c in range(n_chunks)]
    state_in = [[None] * HEADS for _ in range(n_chunks)]
    for h in range(HEADS):
        state = st_s[h]
        for c in range(n_chunks):
            state_in[c][h] = state.astype(BF16)
            state = state * decay_cols[head_cols(h), c:c + 1] + kv_new[c][h]
        st_s[h] = state
    chunk_out = []
    for c in range(n_chunks):
        pairs = []
        for p in range(HEADS // 2):
            outs = []
            for h in (2 * p, 2 * p + 1):
                valid = _valid_lanes(h, CHUNK)
                o = (_dot(scores[c][h], v[rows(c), window(h)])
                     + _dot(qe[rows(c), head_cols(h)], state_in[c][h]))
                ms = jnp.sum(jnp.where(valid, o * o, 0.0), axis=-1, keepdims=True) * (1.0 / GLA_DV)
                y = o * lax.rsqrt(ms + EPS) * gn_ref[:, window(h)]
                outs.append(jnp.where(valid, y, 0.0))
            pairs.append(_merge_pair(outs[0], outs[1]))
        chunk_out.append(jnp.concatenate(pairs, axis=-1))
        fill(1)
    z_g = take(C_G, MIX)
    out_b = jnp.concatenate(chunk_out, axis=0) * (z_g * _sigmoid(z_g))

    mq = take(C_MQ, MIX).astype(BF16)
    parts = []
    for p in range(HEADS // 2):
        outs = []
        for h in (2 * p, 2 * p + 1):
            s = _dot_nt(mq[:, window(h)], mk_ref[0, 0, h]) * (MEM_HEAD_DIM ** -0.5)
            e = jnp.exp(s - jnp.max(s, axis=-1, keepdims=True))
            inv = 1.0 / jnp.sum(e, axis=-1, keepdims=True)
            outs.append(_dot(e.astype(BF16), mv_ref[0, 0, h]) * inv)
            fill(1)
        parts.append(_merge_pair(outs[0], outs[1]))
    out_c = jnp.concatenate(parts, axis=-1)

    merged = None
    for n, br in enumerate((out_a, out_b, out_c)):
        term = _sigmoid(take(C_GATE + n * D_MODEL, D_MODEL)) * _dot(br.astype(BF16), wb_ref[n])
        merged = term if merged is None else merged + term
    o_ref[0] = x + _dot(merged.astype(BF16), wo_ref[...])


def _resident(stacked, layer):
    zeros = (0,) * (stacked.ndim - 1)
    return pl.BlockSpec((None,) + stacked.shape[1:], lambda b, j: (layer,) + zeros,
                        pipeline_mode=pl.Buffered(1))


def _mixer(x, mk, mv, layer, p):
    bsz, slen, _ = x.shape
    t = min(MIX_TILE, slen)
    tile = pl.BlockSpec((1, t, D_MODEL), lambda b, j: (b, j, 0))
    kv_spec = pl.BlockSpec((1, 1, HEADS, MEM_LEN, WIN), lambda b, j: (layer, b, 0, 0, 0))
    weights = (p["gain"], p["w1"], p["cw"], p["cb"], p["wg"], p["bg"], p["lam"], p["wdec"], p["bdec"],
               p["gn"])
    tail = (p["wb"], p["wo"])
    return pl.pallas_call(
        _mixer_kernel,
        grid=(bsz, slen // t),
        in_specs=[tile] + [_resident(w, layer) for w in weights] + [kv_spec, kv_spec]
        + [_resident(w, layer) for w in tail],
        out_specs=tile,
        out_shape=jax.ShapeDtypeStruct(x.shape, F32),
        scratch_shapes=[
            pltpu.VMEM((SUBLANES, MIX), F32),
            pltpu.VMEM((SUBLANES, MIX), F32),
            pltpu.VMEM((HEADS, DK_PAD, WIN), F32),
        ],
        compiler_params=pltpu.CompilerParams(
            dimension_semantics=("arbitrary", "arbitrary"), vmem_limit_bytes=VMEM_LIMIT),
        name="mixer",
    )(x, *weights, mk, mv, *tail)


GELU_C0 = math.sqrt(2.0 / math.pi)
GELU_C1 = GELU_C0 * 0.044715


def _ffn_kernel(x_ref, gain_ref, wup_ref, cw_ref, cb_ref, wdn_ref, fin_ref, o_ref, ut_s, *, final):
    t = x_ref.shape[1]
    j = pl.program_id(1)

    @pl.when(j == 0)
    def _():
        ut_s[...] = jnp.zeros_like(ut_s)

    x = x_ref[0]
    hb = _rms(x, gain_ref[...]).astype(BF16)

    def conv_cols(col, scale):
        cols = slice(col, col + FFN_COLS)
        up = _dot(hb, wup_ref[:, cols])
        tail = ut_s[:, cols]
        cw = cw_ref[:, cols] * scale
        y = cb_ref[:, cols] * scale + cw[FFN_CONV - 1:FFN_CONV, :] * up
        for k in range(1, FFN_CONV):
            y = y + cw[FFN_CONV - 1 - k:FFN_CONV - k, :] * _shift_rows(up, tail, k)
        ut_s[:, cols] = up[t - SUBLANES:, :]
        return y

    acc = x
    for c in range(D_FF // FFN_COLS):
        gate_h = conv_cols(c * FFN_COLS, 1.0)
        half_val = conv_cols(D_FF + c * FFN_COLS, 0.5)
        th = jnp.tanh(gate_h * (GELU_C0 + GELU_C1 * (gate_h * gate_h)))
        act = ((gate_h + gate_h * th) * half_val).astype(BF16)
        acc = acc + _dot(act, wdn_ref[c * FFN_COLS:(c + 1) * FFN_COLS, :])
    o_ref[0] = _rms(acc, fin_ref[...]) if final else acc


def _ffn(x, layer, p, final_gain, final):
    bsz, slen, _ = x.shape
    t = min(FFN_TILE, slen)
    tile = pl.BlockSpec((1, t, D_MODEL), lambda b, j: (b, j, 0))
    weights = (p["gain2"], p["wup"], p["cw2"], p["cb2"], p["wdn"])
    return pl.pallas_call(
        functools.partial(_ffn_kernel, final=final),
        grid=(bsz, slen // t),
        in_specs=[tile] + [_resident(w, layer) for w in weights] + [_resident(final_gain, 0)],
        out_specs=tile,
        out_shape=jax.ShapeDtypeStruct(x.shape, F32),
        scratch_shapes=[pltpu.VMEM((SUBLANES, 2 * D_FF), F32)],
        compiler_params=pltpu.CompilerParams(
            dimension_semantics=("arbitrary", "arbitrary"), vmem_limit_bytes=VMEM_LIMIT),
        name="ffn",
    )(x, *weights, final_gain)


def _pad_heads(w, width, padded):
    lead = w.shape[:-1]
    w = w.reshape(*lead, HEADS, width)
    w = jnp.pad(w, [(0, 0)] * len(lead) + [(0, 0), (0, padded - width)])
    return w.reshape(*lead, HEADS * padded)


def _block_diag(w):
    n, rows, cols = w.shape
    eye = jnp.eye(n, dtype=w.dtype)
    return jnp.einsum("hij,hg->higj", w, eye).reshape(n * rows, n * cols)


def _gate_weights(w_a, w_x):
    per = RNN_BLOCKS // RNN_GROUPS
    return jnp.stack([
        jnp.concatenate([_block_diag(w_a[g * per:(g + 1) * per]),
                         _block_diag(w_x[g * per:(g + 1) * per])], axis=1)
        for g in range(RNN_GROUPS)])


def _pack_weights(w_in, norm_mix, rnn_conv_w, rnn_conv_b, rg_w_a, rg_b_a, rg_w_x, rg_b_x, rg_lambda,
                  gla_w_decay, gla_b_decay, gla_norm, w_branch, w_out, norm_ffn, w_up, ffn_conv_w,
                  ffn_conv_b, w_down):
    depth = w_in.shape[0]
    bounds = [0]
    for width in IN_SPLITS:
        bounds.append(bounds[-1] + width)
    w_r, w_q, w_k, w_v, w_g, w_d, w_mq, w_gate = (
        w_in[:, :, bounds[i]:bounds[i + 1]] for i in range(len(IN_SPLITS)))
    w_q = _pad_heads(w_q, GLA_DK, DK_PAD)
    w_q = w_q.at[:, :, GLA_DK:GLA_DK + GLA_LOW_RANK].set(w_d)
    w1 = jnp.concatenate(
        [w_r, w_q, _pad_heads(w_k, GLA_DK, DK_PAD), w_v, w_g, w_mq, w_gate], axis=2).astype(BF16)
    wdec = jnp.zeros((depth, DK_PAD, HEADS * DK_PAD), F32).at[:, GLA_DK:GLA_DK + GLA_LOW_RANK, :].set(
        _pad_heads(gla_w_decay, GLA_DK, DK_PAD))
    b_a = rg_b_a.reshape(depth, RNN_GROUPS, 1, RNN_GROUP)
    b_x = rg_b_x.reshape(depth, RNN_GROUPS, 1, RNN_GROUP)
    return {
        "gain": norm_mix.reshape(depth, 1, D_MODEL),
        "w1": w1,
        "cw": rnn_conv_w,
        "cb": rnn_conv_b.reshape(depth, 1, MIX),
        "wg": jax.vmap(_gate_weights)(rg_w_a, rg_w_x).astype(BF16),
        "bg": jnp.concatenate([b_a, b_x], axis=-1),
        "lam": rg_lambda.reshape(depth, 1, MIX),
        "wdec": wdec.astype(BF16),
        "bdec": _pad_heads(gla_b_decay.reshape(depth, 1, HEADS * GLA_DK), GLA_DK, DK_PAD),
        "gn": gla_norm.reshape(depth, 1, MIX),
        "wb": w_branch.astype(BF16),
        "wo": w_out.astype(BF16),
        "gain2": norm_ffn.reshape(depth, 1, D_MODEL),
        "wup": w_up.astype(BF16),
        "cw2": ffn_conv_w,
        "cb2": ffn_conv_b.reshape(depth, 1, 2 * D_FF),
        "wdn": w_down.astype(BF16),
    }


def kernel(x, mem, norm_mix, w_in, rnn_conv_w, rnn_conv_b, rg_w_a, rg_b_a, rg_w_x, rg_b_x, rg_lambda,
           gla_w_decay, gla_b_decay, gla_norm, mem_norm, w_mem_kv, w_branch, w_out, norm_ffn, w_up,
           ffn_conv_w, ffn_conv_b, w_down, final_norm):
    depth = w_in.shape[0]
    assert x.shape[1] % CHUNK == 0 and x.shape[2] == D_MODEL and mem.shape[1:] == (MEM_LEN, D_MODEL)
    mk, mv = _memkv(mem, mem_norm, w_mem_kv)
    p = _pack_weights(w_in, norm_mix, rnn_conv_w, rnn_conv_b, rg_w_a, rg_b_a, rg_w_x, rg_b_x, rg_lambda,
                      gla_w_decay, gla_b_decay, gla_norm, w_branch, w_out, norm_ffn, w_up, ffn_conv_w,
                      ffn_conv_b, w_down)
    final_gain = final_norm.reshape(1, 1, D_MODEL)
    for l in range(depth):
        x = _mixer(x, mk, mv, l, p)
        x = _ffn(x, l, p, final_gain, final=(l == depth - 1))
    return x
```

```python
import functools
import math

import jax
import jax.numpy as jnp
from jax import lax
from jax.experimental import pallas as pl
from jax.experimental.pallas import tpu as pltpu

D_MODEL = 1024
MEM_LEN = 256
MIX = 768
N_BRANCHES = 3
RNN_BLOCKS = 8
RNN_BLOCK = MIX // RNN_BLOCKS
RNN_GROUPS = 2
RNN_GROUP = MIX // RNN_GROUPS
RNN_CONV = 4
LRU_C = 8.0
HEADS = 4
GLA_DK = 96
GLA_DV = 192
GLA_LOW_RANK = 16
GLA_NORMALIZER = 16.0
CHUNK = 64
CHUNK_SHIFT = CHUNK.bit_length() - 1
MEM_HEAD_DIM = 192
D_FF = 3 * D_MODEL
FFN_CONV = 3
EPS = 1e-6
IN_SPLITS = (MIX, HEADS * GLA_DK, HEADS * GLA_DK, HEADS * GLA_DV, HEADS * GLA_DV, GLA_LOW_RANK,
             HEADS * MEM_HEAD_DIM, N_BRANCHES * D_MODEL)

LANES = 128
SUBLANES = 8
DK_PAD = LANES
WIN = 2 * LANES
PAIR = 3 * LANES

C_RNN = 0
C_Q = C_RNN + MIX
C_K = C_Q + HEADS * DK_PAD
C_V = C_K + HEADS * DK_PAD
C_G = C_V + MIX
C_MQ = C_G + MIX
C_GATE = C_MQ + MIX
N_IN_PACKED = C_GATE + N_BRANCHES * D_MODEL

MIX_TILE = 512
SCAN_ROWS = 128
FFN_TILE = 512
FFN_COLS = 3072
VMEM_LIMIT = 56 * 1024 * 1024

BF16 = jnp.bfloat16
F32 = jnp.float32


def _dot(a, b):
    return jnp.dot(a, b, preferred_element_type=F32)


def _dot_nt(a, b):
    return lax.dot_general(a, b, (((1,), (1,)), ((), ())), preferred_element_type=F32)


def _dot_tn(a, b):
    return lax.dot_general(a, b, (((0,), (0,)), ((), ())), preferred_element_type=F32)


def _rms(x, gain):
    return x * lax.rsqrt(jnp.mean(x * x, axis=-1, keepdims=True) + EPS) * gain


NEG_LOG2_E = -1.4426950408889634


def _sigmoid(x):
    return 1.0 / (1.0 + jnp.exp2(x * NEG_LOG2_E))


def _log_sigmoid(x):
    return jnp.minimum(x, 0.0) - jnp.log1p(jnp.exp(-jnp.abs(x)))


def _log_sigmoid_wide(x):
    return jnp.minimum(x, 0.0) - jnp.log(1.0 + jnp.exp(-jnp.abs(x)))


def _sqrt_nonneg(y):
    return jnp.where(y > 0.0, y * lax.rsqrt(y), 0.0)


def _shift_rows(x, tail, k):
    rolled = pltpu.roll(x, k, 0)
    row = lax.broadcasted_iota(jnp.int32, (SUBLANES, x.shape[1]), 0)
    top = jnp.where(row < k, pltpu.roll(tail, k, 0), rolled[:SUBLANES])
    return jnp.concatenate([top, rolled[SUBLANES:]], axis=0)


def _head_window(h):
    pair, odd = divmod(h, 2)
    start = pair * PAIR + odd * LANES
    lo = odd * (GLA_DV - LANES)
    return start, lo, lo + GLA_DV


def _valid_lanes(h, rows):
    _, lo, hi = _head_window(h)
    lane = lax.broadcasted_iota(jnp.int32, (rows, WIN), 1)
    return (lane >= lo) & (lane < hi)


def _merge_pair(even, odd):
    return jnp.concatenate(
        [even[:, :LANES], even[:, LANES:] + odd[:, :LANES], odd[:, LANES:]], axis=-1)


def _memkv_kernel(mem_ref, gain_ref, w_ref, mk_ref, mv_ref):
    m = _rms(mem_ref[0], gain_ref[0]).astype(BF16)
    kv = _dot(m, w_ref[0])
    for h in range(HEADS):
        start, _, _ = _head_window(h)
        valid = _valid_lanes(h, MEM_LEN)
        mk_ref[0, 0, h] = jnp.where(valid, kv[:, start:start + WIN], 0.0).astype(BF16)
        mv_ref[0, 0, h] = jnp.where(valid, kv[:, MIX + start:MIX + start + WIN], 0.0).astype(BF16)


def _memkv(mem, mem_norm, w_mem_kv):
    depth = w_mem_kv.shape[0]
    bsz = mem.shape[0]
    out = jax.ShapeDtypeStruct((depth, bsz, HEADS, MEM_LEN, WIN), BF16)
    out_spec = pl.BlockSpec((1, 1, HEADS, MEM_LEN, WIN), lambda l, b: (l, b, 0, 0, 0))
    return pl.pallas_call(
        _memkv_kernel,
        grid=(depth, bsz),
        in_specs=[
            pl.BlockSpec((1, MEM_LEN, D_MODEL), lambda l, b: (b, 0, 0)),
            pl.BlockSpec((1, 1, D_MODEL), lambda l, b: (l, 0, 0)),
            pl.BlockSpec((1, D_MODEL, 2 * MIX), lambda l, b: (l, 0, 0)),
        ],
        out_specs=[out_spec, out_spec],
        out_shape=[out, out],
        compiler_params=pltpu.CompilerParams(
            dimension_semantics=("arbitrary", "arbitrary"), vmem_limit_bytes=VMEM_LIMIT),
        name="memkv",
    )(mem, mem_norm.reshape(depth, 1, D_MODEL), w_mem_kv.astype(BF16))


def _mixer_kernel(x_ref, gain_ref, w1_ref, cw_ref, cb_ref, wg_ref, bg_ref, lam_ref, wdec_ref,
                  bdec_ref, gn_ref, mk_ref, mv_ref, wb_ref, wo_ref, o_ref,
                  zt_s, hp_s, st_s):
    t = x_ref.shape[1]
    j = pl.program_id(1)

    @pl.when(j == 0)
    def _():
        zt_s[...] = jnp.zeros_like(zt_s)
        hp_s[...] = jnp.zeros_like(hp_s)
        st_s[...] = jnp.zeros_like(st_s)

    x = x_ref[0]
    hb = _rms(x, gain_ref[...]).astype(BF16)

    def proj(col, width):
        return _dot(hb, w1_ref[:, col:col + width])

    queue = [col + i * WIN for col, width in ((C_Q, HEADS * DK_PAD), (C_K, HEADS * DK_PAD), (C_V, MIX),
                                              (C_MQ, MIX), (C_G, MIX), (C_GATE, N_BRANCHES * D_MODEL))
             for i in range(width // WIN)]
    ready = {}

    def fill(n):
        for _ in range(min(n, len(queue))):
            col = queue.pop(0)
            ready[col] = proj(col, WIN)

    def take(col, width):
        cols = [col + i * WIN for i in range(width // WIN)]
        while any(c not in ready for c in cols):
            fill(1)
        return jnp.concatenate([ready.pop(c) for c in cols], axis=-1)

    rb = min(SCAN_ROWS, t)
    blocks = [slice(i * rb, (i + 1) * rb) for i in range(t // rb)]
    z_r = proj(C_RNN, MIX)
    xr_parts = []
    tail = zt_s[...]
    for i, rows_b in enumerate(blocks):
        z_b = z_r[rows_b]
        xr_b = cb_ref[...] + cw_ref[RNN_CONV - 1:RNN_CONV, :] * z_b
        for k in range(1, RNN_CONV):
            xr_b = xr_b + cw_ref[RNN_CONV - 1 - k:RNN_CONV - k, :] * _shift_rows(z_b, tail, k)
        xr_parts.append(xr_b)
        tail = z_b[rb - SUBLANES:]
        fill(i % 2)
    zt_s[...] = tail
    xr = jnp.concatenate(xr_parts, axis=0)

    xr_bf = xr.astype(BF16)
    r_parts, i_parts = [], []
    for grp in range(RNN_GROUPS):
        lanes = slice(grp * RNN_GROUP, (grp + 1) * RNN_GROUP)
        gates = _dot(xr_bf[:, lanes], wg_ref[grp]) + bg_ref[grp]
        r_parts.append(gates[:, :RNN_GROUP])
        i_parts.append(gates[:, RNN_GROUP:])
    r_pre = jnp.concatenate(r_parts, axis=-1)
    i_pre = jnp.concatenate(i_parts, axis=-1)
    decay_rate = LRU_C * _log_sigmoid(lam_ref[...])
    row8 = lax.broadcasted_iota(jnp.int32, (SUBLANES, MIX), 0)
    zeros8 = jnp.zeros((SUBLANES, MIX), F32)
    ones8 = jnp.ones((SUBLANES, MIX), F32)
    h_prev = hp_s[0:1, :]
    out_parts = []
    for rows_b, xr_b in zip(blocks, xr_parts):
        a = jnp.exp(decay_rate * _sigmoid(r_pre[rows_b]))
        u = _sqrt_nonneg(1.0 - a * a) * (_sigmoid(i_pre[rows_b]) * xr_b)
        fill(1)
        u = jnp.concatenate(
            [u[:SUBLANES] + jnp.where(row8 == 0, a[:SUBLANES] * h_prev, 0.0), u[SUBLANES:]], axis=0)
        k = 1
        while k < rb:
            if k < SUBLANES:
                u_prev = _shift_rows(u, zeros8, k)
                a_prev = _shift_rows(a, ones8, k)
            else:
                u_prev = jnp.concatenate([jnp.zeros((k, MIX), F32), u[:rb - k]], axis=0)
                a_prev = jnp.concatenate([jnp.ones((k, MIX), F32), a[:rb - k]], axis=0)
            u = u + a * u_prev
            if 2 * k < rb:
                a = a * a_prev
            if k == SUBLANES // 2:
                fill(1)
            k *= 2
        out_parts.append(u)
        h_prev = u[rb - 1:rb, :]
    out_a = jnp.concatenate(out_parts, axis=0)
    hp_s[0:1, :] = h_prev

    q = take(C_Q, HEADS * DK_PAD)
    kk = take(C_K, HEADS * DK_PAD)
    v = take(C_V, MIX).astype(BF16)
    logit = _dot(q[:, :DK_PAD].astype(BF16), wdec_ref[...]) + bdec_ref[...]
    g = _log_sigmoid_wide(logit) * (1.0 / GLA_NORMALIZER)
    g_hi = g.astype(BF16)
    g_lo = (g - g_hi.astype(F32)).astype(BF16)
    cb = min(WIN, t)
    ri = lax.broadcasted_iota(jnp.int32, (cb, cb), 0)
    ci = lax.broadcasted_iota(jnp.int32, (cb, cb), 1)
    same_chunk = lax.shift_right_logical(ri, CHUNK_SHIFT) == lax.shift_right_logical(ci, CHUNK_SHIFT)
    m_cum = jnp.where(same_chunk & (ci <= ri), 1.0, 0.0).astype(BF16)
    g_cum = jnp.concatenate(
        [_dot(m_cum, g_hi[i * cb:(i + 1) * cb]) + _dot(m_cum, g_lo[i * cb:(i + 1) * cb])
         for i in range(t // cb)], axis=0)
    n_chunks = t // CHUNK
    g_tot = jnp.concatenate(
        [jnp.broadcast_to(g_cum[(c + 1) * CHUNK - 1:(c + 1) * CHUNK, :], (CHUNK, HEADS * DK_PAD))
         for c in range(n_chunks)], axis=0)
    qe = (q * ((GLA_DK ** -0.5) * jnp.exp(g_cum))).astype(BF16)
    fill(1)
    ke = (kk * jnp.exp(-g_cum)).astype(BF16)
    fill(1)
    kd = (kk * jnp.exp(g_tot - g_cum)).astype(BF16)
    fill(1)
    sel = (lax.shift_right_logical(lax.broadcasted_iota(jnp.int32, (t, LANES), 0), CHUNK_SHIFT)
           == lax.broadcasted_iota(jnp.int32, (t, LANES), 1))
    sel = jnp.where(sel, 1.0, 0.0).astype(BF16)
    decay_cols = jnp.exp(_dot_tn(g_hi, sel) + _dot_tn(g_lo, sel))

    cr = lax.broadcasted_iota(jnp.int32, (CHUNK, CHUNK), 0)
    cc = lax.broadcasted_iota(jnp.int32, (CHUNK, CHUNK), 1)
    causal = cc <= cr

    def rows(c):
        return slice(c * CHUNK, (c + 1) * CHUNK)

    def head_cols(h):
        return slice(h * DK_PAD, (h + 1) * DK_PAD)

    def window(h):
        start, _, _ = _head_window(h)
        return slice(start, start + WIN)

    scores = [[jnp.where(causal, _dot_nt(qe[rows(c), head_cols(h)], ke[rows(c), head_cols(h)]), 0.0)
               .astype(BF16) for h in range(HEADS)] for c in range(n_chunks)]
    kv_new = [[_dot_tn(kd[rows(c), head_cols(h)], v[rows(c), window(h)]) for h in range(HEADS)]
              for c in range(n_chunks)]
    state_in = [[None] * HEADS for _ in range(n_chunks)]
    for h in range(HEADS):
        state = st_s[h]
        for c in range(n_chunks):
            state_in[c][h] = state.astype(BF16)
            state = state * decay_cols[head_cols(h), c:c + 1] + kv_new[c][h]
        st_s[h] = state
    chunk_out = []
    for c in range(n_chunks):
        pairs = []
        for p in range(HEADS // 2):
            outs = []
            for h in (2 * p, 2 * p + 1):
                valid = _valid_lanes(h, CHUNK)
                o = (_dot(scores[c][h], v[rows(c), window(h)])
                     + _dot(qe[rows(c), head_cols(h)], state_in[c][h]))
                ms = jnp.sum(jnp.where(valid, o * o, 0.0), axis=-1, keepdims=True) * (1.0 / GLA_DV)
                y = o * lax.rsqrt(ms + EPS) * gn_ref[:, window(h)]
                outs.append(jnp.where(valid, y, 0.0))
            pairs.append(_merge_pair(outs[0], outs[1]))
        chunk_out.append(jnp.concatenate(pairs, axis=-1))
        fill(1)
    z_g = take(C_G, MIX)
    out_b = jnp.concatenate(chunk_out, axis=0) * (z_g * _sigmoid(z_g))

    mq = take(C_MQ, MIX).astype(BF16)
    parts = []
    for p in range(HEADS // 2):
        outs = []
        for h in (2 * p, 2 * p + 1):
            s = _dot_nt(mq[:, window(h)], mk_ref[0, 0, h]) * (MEM_HEAD_DIM ** -0.5)
            e = jnp.exp(s - jnp.max(s, axis=-1, keepdims=True))
            inv = 1.0 / jnp.sum(e, axis=-1, keepdims=True)
            outs.append(_dot(e.astype(BF16), mv_ref[0, 0, h]) * inv)
            fill(1)
        parts.append(_merge_pair(outs[0], outs[1]))
    out_c = jnp.concatenate(parts, axis=-1)

    merged = None
    for n, br in enumerate((out_a, out_b, out_c)):
        term = _sigmoid(take(C_GATE + n * D_MODEL, D_MODEL)) * _dot(br.astype(BF16), wb_ref[n])
        merged = term if merged is None else merged + term
    o_ref[0] = x + _dot(merged.astype(BF16), wo_ref[...])


def _resident(stacked, layer):
    zeros = (0,) * (stacked.ndim - 1)
    return pl.BlockSpec((None,) + stacked.shape[1:], lambda b, j: (layer,) + zeros,
                        pipeline_mode=pl.Buffered(1))


def _mixer(x, mk, mv, layer, p):
    bsz, slen, _ = x.shape
    t = min(MIX_TILE, slen)
    tile = pl.BlockSpec((1, t, D_MODEL), lambda b, j: (b, j, 0))
    kv_spec = pl.BlockSpec((1, 1, HEADS, MEM_LEN, WIN), lambda b, j: (layer, b, 0, 0, 0))
    weights = (p["gain"], p["w1"], p["cw"], p["cb"], p["wg"], p["bg"], p["lam"], p["wdec"], p["bdec"],
               p["gn"])
    tail = (p["wb"], p["wo"])
    return pl.pallas_call(
        _mixer_kernel,
        grid=(bsz, slen // t),
        in_specs=[tile] + [_resident(w, layer) for w in weights] + [kv_spec, kv_spec]
        + [_resident(w, layer) for w in tail],
        out_specs=tile,
        out_shape=jax.ShapeDtypeStruct(x.shape, F32),
        scratch_shapes=[
            pltpu.VMEM((SUBLANES, MIX), F32),
            pltpu.VMEM((SUBLANES, MIX), F32),
            pltpu.VMEM((HEADS, DK_PAD, WIN), F32),
        ],
        compiler_params=pltpu.CompilerParams(
            dimension_semantics=("arbitrary", "arbitrary"), vmem_limit_bytes=VMEM_LIMIT),
        name="mixer",
    )(x, *weights, mk, mv, *tail)


GELU_C0 = math.sqrt(2.0 / math.pi)
GELU_C1 = GELU_C0 * 0.044715


def _ffn_kernel(x_ref, gain_ref, wup_ref, cw_ref, cb_ref, wdn_ref, fin_ref, o_ref, ut_s, *, final):
    t = x_ref.shape[1]
    j = pl.program_id(1)

    @pl.when(j == 0)
    def _():
        ut_s[...] = jnp.zeros_like(ut_s)

    x = x_ref[0]
    hb = _rms(x, gain_ref[...]).astype(BF16)

    def conv_cols(col, scale):
        cols = slice(col, col + FFN_COLS)
        up = _dot(hb, wup_ref[:, cols])
        tail = ut_s[:, cols]
        cw = cw_ref[:, cols] * scale
        y = cb_ref[:, cols] * scale + cw[FFN_CONV - 1:FFN_CONV, :] * up
        for k in range(1, FFN_CONV):
            y = y + cw[FFN_CONV - 1 - k:FFN_CONV - k, :] * _shift_rows(up, tail, k)
        ut_s[:, cols] = up[t - SUBLANES:, :]
        return y

    acc = x
    for c in range(D_FF // FFN_COLS):
        gate_h = conv_cols(c * FFN_COLS, 1.0)
        half_val = conv_cols(D_FF + c * FFN_COLS, 0.5)
        th = jnp.tanh(gate_h * (GELU_C0 + GELU_C1 * (gate_h * gate_h)))
        act = ((gate_h + gate_h * th) * half_val).astype(BF16)
        acc = acc + _dot(act, wdn_ref[c * FFN_COLS:(c + 1) * FFN_COLS, :])
    o_ref[0] = _rms(acc, fin_ref[...]) if final else acc


def _ffn(x, layer, p, final_gain, final):
    bsz, slen, _ = x.shape
    t = min(FFN_TILE, slen)
    tile = pl.BlockSpec((1, t, D_MODEL), lambda b, j: (b, j, 0))
    weights = (p["gain2"], p["wup"], p["cw2"], p["cb2"], p["wdn"])
    return pl.pallas_call(
        functools.partial(_ffn_kernel, final=final),
        grid=(bsz, slen // t),
        in_specs=[tile] + [_resident(w, layer) for w in weights] + [_resident(final_gain, 0)],
        out_specs=tile,
        out_shape=jax.ShapeDtypeStruct(x.shape, F32),
        scratch_shapes=[pltpu.VMEM((SUBLANES, 2 * D_FF), F32)],
        compiler_params=pltpu.CompilerParams(
            dimension_semantics=("arbitrary", "arbitrary"), vmem_limit_bytes=VMEM_LIMIT),
        name="ffn",
    )(x, *weights, final_gain)


def _pad_heads(w, width, padded):
    lead = w.shape[:-1]
    w = w.reshape(*lead, HEADS, width)
    w = jnp.pad(w, [(0, 0)] * len(lead) + [(0, 0), (0, padded - width)])
    return w.reshape(*lead, HEADS * padded)


def _block_diag(w):
    n, rows, cols = w.shape
    eye = jnp.eye(n, dtype=w.dtype)
    return jnp.einsum("hij,hg->higj", w, eye).reshape(n * rows, n * cols)


def _gate_weights(w_a, w_x):
    per = RNN_BLOCKS // RNN_GROUPS
    return jnp.stack([
        jnp.concatenate([_block_diag(w_a[g * per:(g + 1) * per]),
                         _block_diag(w_x[g * per:(g + 1) * per])], axis=1)
        for g in range(RNN_GROUPS)])


def _pack_weights(w_in, norm_mix, rnn_conv_w, rnn_conv_b, rg_w_a, rg_b_a, rg_w_x, rg_b_x, rg_lambda,
                  gla_w_decay, gla_b_decay, gla_norm, w_branch, w_out, norm_ffn, w_up, ffn_conv_w,
                  ffn_conv_b, w_down):
    depth = w_in.shape[0]
    bounds = [0]
    for width in IN_SPLITS:
        bounds.append(bounds[-1] + width)
    w_r, w_q, w_k, w_v, w_g, w_d, w_mq, w_gate = (
        w_in[:, :, bounds[i]:bounds[i + 1]] for i in range(len(IN_SPLITS)))
    spare = DK_PAD - GLA_DK - GLA_LOW_RANK
    q_heads = w_q.astype(BF16).reshape(depth, D_MODEL, HEADS, GLA_DK)
    q_cols = [q_heads[:, :, 0], w_d.astype(BF16), jnp.zeros((depth, D_MODEL, spare), BF16)]
    for h in range(1, HEADS):
        q_cols += [q_heads[:, :, h], jnp.zeros((depth, D_MODEL, DK_PAD - GLA_DK), BF16)]
    w1 = jnp.concatenate(
        [w_r.astype(BF16)] + q_cols + [_pad_heads(w_k.astype(BF16), GLA_DK, DK_PAD)]
        + [w.astype(BF16) for w in (w_v, w_g, w_mq, w_gate)], axis=2)
    wdec = jnp.pad(_pad_heads(gla_w_decay, GLA_DK, DK_PAD), ((0, 0), (GLA_DK, spare), (0, 0)))
    b_a = rg_b_a.reshape(depth, RNN_GROUPS, 1, RNN_GROUP)
    b_x = rg_b_x.reshape(depth, RNN_GROUPS, 1, RNN_GROUP)
    return {
        "gain": norm_mix.reshape(depth, 1, D_MODEL),
        "w1": w1,
        "cw": rnn_conv_w,
        "cb": rnn_conv_b.reshape(depth, 1, MIX),
        "wg": jax.vmap(_gate_weights)(rg_w_a, rg_w_x).astype(BF16),
        "bg": jnp.concatenate([b_a, b_x], axis=-1),
        "lam": rg_lambda.reshape(depth, 1, MIX),
        "wdec": wdec.astype(BF16),
        "bdec": _pad_heads(gla_b_decay.reshape(depth, 1, HEADS * GLA_DK), GLA_DK, DK_PAD),
        "gn": gla_norm.reshape(depth, 1, MIX),
        "wb": w_branch.astype(BF16),
        "wo": w_out.astype(BF16),
        "gain2": norm_ffn.reshape(depth, 1, D_MODEL),
        "wup": w_up.astype(BF16),
        "cw2": ffn_conv_w,
        "cb2": ffn_conv_b.reshape(depth, 1, 2 * D_FF),
        "wdn": w_down.astype(BF16),
    }


def kernel(x, mem, norm_mix, w_in, rnn_conv_w, rnn_conv_b, rg_w_a, rg_b_a, rg_w_x, rg_b_x, rg_lambda,
           gla_w_decay, gla_b_decay, gla_norm, mem_norm, w_mem_kv, w_branch, w_out, norm_ffn, w_up,
           ffn_conv_w, ffn_conv_b, w_down, final_norm):
    depth = w_in.shape[0]
    assert x.shape[1] % CHUNK == 0 and x.shape[2] == D_MODEL and mem.shape[1:] == (MEM_LEN, D_MODEL)
    mk, mv = _memkv(mem, mem_norm, w_mem_kv)
    p = _pack_weights(w_in, norm_mix, rnn_conv_w, rnn_conv_b, rg_w_a, rg_b_a, rg_w_x, rg_b_x, rg_lambda,
                      gla_w_decay, gla_b_decay, gla_norm, w_branch, w_out, norm_ffn, w_up, ffn_conv_w,
                      ffn_conv_b, w_down)
    final_gain = final_norm.reshape(1, 1, D_MODEL)
    for l in range(depth):
        x = _mixer(x, mk, mv, l, p)
        x = _ffn(x, l, p, final_gain, final=(l == depth - 1))
    return x
```

```python
import functools
import math

import jax
import jax.numpy as jnp
from jax import lax
from jax.experimental import pallas as pl
from jax.experimental.pallas import tpu as pltpu

D_MODEL = 1024
MEM_LEN = 256
MIX = 768
N_BRANCHES = 3
RNN_BLOCKS = 8
RNN_BLOCK = MIX // RNN_BLOCKS
RNN_GROUPS = 2
RNN_GROUP = MIX // RNN_GROUPS
RNN_CONV = 4
LRU_C = 8.0
HEADS = 4
GLA_DK = 96
GLA_DV = 192
GLA_LOW_RANK = 16
GLA_NORMALIZER = 16.0
CHUNK = 64
CHUNK_SHIFT = CHUNK.bit_length() - 1
MEM_HEAD_DIM = 192
D_FF = 3 * D_MODEL
FFN_CONV = 3
EPS = 1e-6
IN_SPLITS = (MIX, HEADS * GLA_DK, HEADS * GLA_DK, HEADS * GLA_DV, HEADS * GLA_DV, GLA_LOW_RANK,
             HEADS * MEM_HEAD_DIM, N_BRANCHES * D_MODEL)

LANES = 128
SUBLANES = 8
DK_PAD = LANES
WIN = 2 * LANES
PAIR = 3 * LANES

C_RNN = 0
C_Q = C_RNN + MIX
C_K = C_Q + HEADS * DK_PAD
C_V = C_K + HEADS * DK_PAD
C_G = C_V + MIX
C_MQ = C_G + MIX
C_GATE = C_MQ + MIX
N_IN_PACKED = C_GATE + N_BRANCHES * D_MODEL

MIX_TILE = 512
SCAN_ROWS = 128
FFN_TILE = 1024
FFN_COLS = 512
VMEM_LIMIT = 56 * 1024 * 1024

BF16 = jnp.bfloat16
F32 = jnp.float32


def _dot(a, b):
    return jnp.dot(a, b, preferred_element_type=F32)


def _dot_nt(a, b):
    return lax.dot_general(a, b, (((1,), (1,)), ((), ())), preferred_element_type=F32)


def _dot_tn(a, b):
    return lax.dot_general(a, b, (((0,), (0,)), ((), ())), preferred_element_type=F32)


def _rms(x, gain):
    return x * lax.rsqrt(jnp.mean(x * x, axis=-1, keepdims=True) + EPS) * gain


NEG_LOG2_E = -1.4426950408889634


def _sigmoid(x):
    return 1.0 / (1.0 + jnp.exp2(x * NEG_LOG2_E))


def _log_sigmoid(x):
    return jnp.minimum(x, 0.0) - jnp.log1p(jnp.exp(-jnp.abs(x)))


def _log_sigmoid_wide(x):
    return jnp.minimum(x, 0.0) - jnp.log(1.0 + jnp.exp(-jnp.abs(x)))


def _sqrt_nonneg(y):
    return jnp.where(y > 0.0, y * lax.rsqrt(y), 0.0)


def _shift_rows(x, tail, k):
    rolled = pltpu.roll(x, k, 0)
    row = lax.broadcasted_iota(jnp.int32, (SUBLANES, x.shape[1]), 0)
    top = jnp.where(row < k, pltpu.roll(tail, k, 0), rolled[:SUBLANES])
    return jnp.concatenate([top, rolled[SUBLANES:]], axis=0)


def _head_window(h):
    pair, odd = divmod(h, 2)
    start = pair * PAIR + odd * LANES
    lo = odd * (GLA_DV - LANES)
    return start, lo, lo + GLA_DV


def _valid_lanes(h, rows):
    _, lo, hi = _head_window(h)
    lane = lax.broadcasted_iota(jnp.int32, (rows, WIN), 1)
    return (lane >= lo) & (lane < hi)


def _merge_pair(even, odd):
    return jnp.concatenate(
        [even[:, :LANES], even[:, LANES:] + odd[:, :LANES], odd[:, LANES:]], axis=-1)


def _memkv_kernel(mem_ref, gain_ref, w_ref, mk_ref, mv_ref):
    m = _rms(mem_ref[0], gain_ref[0]).astype(BF16)
    kv = _dot(m, w_ref[0])
    for h in range(HEADS):
        start, _, _ = _head_window(h)
        valid = _valid_lanes(h, MEM_LEN)
        mk_ref[0, 0, h] = jnp.where(valid, kv[:, start:start + WIN], 0.0).astype(BF16)
        mv_ref[0, 0, h] = jnp.where(valid, kv[:, MIX + start:MIX + start + WIN], 0.0).astype(BF16)


def _memkv(mem, mem_norm, w_mem_kv):
    depth = w_mem_kv.shape[0]
    bsz = mem.shape[0]
    out = jax.ShapeDtypeStruct((depth, bsz, HEADS, MEM_LEN, WIN), BF16)
    out_spec = pl.BlockSpec((1, 1, HEADS, MEM_LEN, WIN), lambda l, b: (l, b, 0, 0, 0))
    return pl.pallas_call(
        _memkv_kernel,
        grid=(depth, bsz),
        in_specs=[
            pl.BlockSpec((1, MEM_LEN, D_MODEL), lambda l, b: (b, 0, 0)),
            pl.BlockSpec((1, 1, D_MODEL), lambda l, b: (l, 0, 0)),
            pl.BlockSpec((1, D_MODEL, 2 * MIX), lambda l, b: (l, 0, 0)),
        ],
        out_specs=[out_spec, out_spec],
        out_shape=[out, out],
        compiler_params=pltpu.CompilerParams(
            dimension_semantics=("arbitrary", "arbitrary"), vmem_limit_bytes=VMEM_LIMIT),
        name="memkv",
    )(mem, mem_norm.reshape(depth, 1, D_MODEL), w_mem_kv.astype(BF16))


def _mixer_kernel(x_ref, gain_ref, w1_ref, cw_ref, cb_ref, wg_ref, bg_ref, lam_ref, wdec_ref,
                  bdec_ref, gn_ref, mk_ref, mv_ref, wb_ref, wo_ref, o_ref,
                  zt_s, hp_s, st_s):
    t = x_ref.shape[1]
    j = pl.program_id(1)

    @pl.when(j == 0)
    def _():
        zt_s[...] = jnp.zeros_like(zt_s)
        hp_s[...] = jnp.zeros_like(hp_s)
        st_s[...] = jnp.zeros_like(st_s)

    x = x_ref[0]
    hb = _rms(x, gain_ref[...]).astype(BF16)

    def proj(col, width):
        return _dot(hb, w1_ref[:, col:col + width])

    queue = [col + i * WIN for col, width in ((C_Q, HEADS * DK_PAD), (C_K, HEADS * DK_PAD), (C_V, MIX),
                                              (C_MQ, MIX), (C_G, MIX), (C_GATE, N_BRANCHES * D_MODEL))
             for i in range(width // WIN)]
    ready = {}

    def fill(n):
        for _ in range(min(n, len(queue))):
            col = queue.pop(0)
            ready[col] = proj(col, WIN)

    def take(col, width):
        cols = [col + i * WIN for i in range(width // WIN)]
        while any(c not in ready for c in cols):
            fill(1)
        return jnp.concatenate([ready.pop(c) for c in cols], axis=-1)

    rb = min(SCAN_ROWS, t)
    blocks = [slice(i * rb, (i + 1) * rb) for i in range(t // rb)]
    z_r = proj(C_RNN, MIX)
    xr_parts = []
    tail = zt_s[...]
    for i, rows_b in enumerate(blocks):
        z_b = z_r[rows_b]
        xr_b = cb_ref[...] + cw_ref[RNN_CONV - 1:RNN_CONV, :] * z_b
        for k in range(1, RNN_CONV):
            xr_b = xr_b + cw_ref[RNN_CONV - 1 - k:RNN_CONV - k, :] * _shift_rows(z_b, tail, k)
        xr_parts.append(xr_b)
        tail = z_b[rb - SUBLANES:]
        fill(i % 2)
    zt_s[...] = tail
    xr = jnp.concatenate(xr_parts, axis=0)

    xr_bf = xr.astype(BF16)
    r_parts, i_parts = [], []
    for grp in range(RNN_GROUPS):
        lanes = slice(grp * RNN_GROUP, (grp + 1) * RNN_GROUP)
        gates = _dot(xr_bf[:, lanes], wg_ref[grp]) + bg_ref[grp]
        r_parts.append(gates[:, :RNN_GROUP])
        i_parts.append(gates[:, RNN_GROUP:])
    r_pre = jnp.concatenate(r_parts, axis=-1)
    i_pre = jnp.concatenate(i_parts, axis=-1)
    decay_rate = LRU_C * _log_sigmoid(lam_ref[...])
    row8 = lax.broadcasted_iota(jnp.int32, (SUBLANES, MIX), 0)
    zeros8 = jnp.zeros((SUBLANES, MIX), F32)
    ones8 = jnp.ones((SUBLANES, MIX), F32)
    h_prev = hp_s[0:1, :]
    out_parts = []
    for rows_b, xr_b in zip(blocks, xr_parts):
        a = jnp.exp(decay_rate * _sigmoid(r_pre[rows_b]))
        u = _sqrt_nonneg(1.0 - a * a) * (_sigmoid(i_pre[rows_b]) * xr_b)
        fill(1)
        u = jnp.concatenate(
            [u[:SUBLANES] + jnp.where(row8 == 0, a[:SUBLANES] * h_prev, 0.0), u[SUBLANES:]], axis=0)
        k = 1
        while k < rb:
            if k < SUBLANES:
                u_prev = _shift_rows(u, zeros8, k)
                a_prev = _shift_rows(a, ones8, k)
            else:
                u_prev = jnp.concatenate([jnp.zeros((k, MIX), F32), u[:rb - k]], axis=0)
                a_prev = jnp.concatenate([jnp.ones((k, MIX), F32), a[:rb - k]], axis=0)
            u = u + a * u_prev
            if 2 * k < rb:
                a = a * a_prev
            if k == SUBLANES // 2:
                fill(1)
            k *= 2
        out_parts.append(u)
        h_prev = u[rb - 1:rb, :]
    out_a = jnp.concatenate(out_parts, axis=0)
    hp_s[0:1, :] = h_prev

    q = take(C_Q, HEADS * DK_PAD)
    kk = take(C_K, HEADS * DK_PAD)
    v = take(C_V, MIX).astype(BF16)
    logit = _dot(q[:, :DK_PAD].astype(BF16), wdec_ref[...]) + bdec_ref[...]
    g = _log_sigmoid_wide(logit) * (1.0 / GLA_NORMALIZER)
    g_hi = g.astype(BF16)
    g_lo = (g - g_hi.astype(F32)).astype(BF16)
    cb = min(WIN, t)
    ri = lax.broadcasted_iota(jnp.int32, (cb, cb), 0)
    ci = lax.broadcasted_iota(jnp.int32, (cb, cb), 1)
    same_chunk = lax.shift_right_logical(ri, CHUNK_SHIFT) == lax.shift_right_logical(ci, CHUNK_SHIFT)
    m_cum = jnp.where(same_chunk & (ci <= ri), 1.0, 0.0).astype(BF16)
    g_cum = jnp.concatenate(
        [_dot(m_cum, g_hi[i * cb:(i + 1) * cb]) + _dot(m_cum, g_lo[i * cb:(i + 1) * cb])
         for i in range(t // cb)], axis=0)
    n_chunks = t // CHUNK
    g_tot = jnp.concatenate(
        [jnp.broadcast_to(g_cum[(c + 1) * CHUNK - 1:(c + 1) * CHUNK, :], (CHUNK, HEADS * DK_PAD))
         for c in range(n_chunks)], axis=0)
    qe = (q * ((GLA_DK ** -0.5) * jnp.exp(g_cum))).astype(BF16)
    fill(1)
    ke = (kk * jnp.exp(-g_cum)).astype(BF16)
    fill(1)
    kd = (kk * jnp.exp(g_tot - g_cum)).astype(BF16)
    fill(1)
    sel = (lax.shift_right_logical(lax.broadcasted_iota(jnp.int32, (t, LANES), 0), CHUNK_SHIFT)
           == lax.broadcasted_iota(jnp.int32, (t, LANES), 1))
    sel = jnp.where(sel, 1.0, 0.0).astype(BF16)
    decay_cols = jnp.exp(_dot_tn(g_hi, sel) + _dot_tn(g_lo, sel))

    cr = lax.broadcasted_iota(jnp.int32, (CHUNK, CHUNK), 0)
    cc = lax.broadcasted_iota(jnp.int32, (CHUNK, CHUNK), 1)
    causal = cc <= cr

    def rows(c):
        return slice(c * CHUNK, (c + 1) * CHUNK)

    def head_cols(h):
        return slice(h * DK_PAD, (h + 1) * DK_PAD)

    def window(h):
        start, _, _ = _head_window(h)
        return slice(start, start + WIN)

    scores = [[jnp.where(causal, _dot_nt(qe[rows(c), head_cols(h)], ke[rows(c), head_cols(h)]), 0.0)
               .astype(BF16) for h in range(HEADS)] for c in range(n_chunks)]
    kv_new = [[_dot_tn(kd[rows(c), head_cols(h)], v[rows(c), window(h)]) for h in range(HEADS)]
              for c in range(n_chunks)]
    state_in = [[None] * HEADS for _ in range(n_chunks)]
    for h in range(HEADS):
        state = st_s[h]
        for c in range(n_chunks):
            state_in[c][h] = state.astype(BF16)
            state = state * decay_cols[head_cols(h), c:c + 1] + kv_new[c][h]
        st_s[h] = state
    chunk_out = []
    for c in range(n_chunks):
        pairs = []
        for p in range(HEADS // 2):
            outs = []
            for h in (2 * p, 2 * p + 1):
                valid = _valid_lanes(h, CHUNK)
                o = (_dot(scores[c][h], v[rows(c), window(h)])
                     + _dot(qe[rows(c), head_cols(h)], state_in[c][h]))
                ms = jnp.sum(jnp.where(valid, o * o, 0.0), axis=-1, keepdims=True) * (1.0 / GLA_DV)
                y = o * lax.rsqrt(ms + EPS) * gn_ref[:, window(h)]
                outs.append(jnp.where(valid, y, 0.0))
            pairs.append(_merge_pair(outs[0], outs[1]))
        chunk_out.append(jnp.concatenate(pairs, axis=-1))
        fill(1)
    z_g = take(C_G, MIX)
    out_b = jnp.concatenate(chunk_out, axis=0) * (z_g * _sigmoid(z_g))

    mq = take(C_MQ, MIX).astype(BF16)
    parts = []
    for p in range(HEADS // 2):
        outs = []
        for h in (2 * p, 2 * p + 1):
            s = _dot_nt(mq[:, window(h)], mk_ref[0, 0, h]) * (MEM_HEAD_DIM ** -0.5)
            e = jnp.exp(s - jnp.max(s, axis=-1, keepdims=True))
            inv = 1.0 / jnp.sum(e, axis=-1, keepdims=True)
            outs.append(_dot(e.astype(BF16), mv_ref[0, 0, h]) * inv)
            fill(1)
        parts.append(_merge_pair(outs[0], outs[1]))
    out_c = jnp.concatenate(parts, axis=-1)

    merged = None
    for n, br in enumerate((out_a, out_b, out_c)):
        term = _sigmoid(take(C_GATE + n * D_MODEL, D_MODEL)) * _dot(br.astype(BF16), wb_ref[n])
        merged = term if merged is None else merged + term
    o_ref[0] = x + _dot(merged.astype(BF16), wo_ref[...])


def _resident(stacked, layer):
    zeros = (0,) * (stacked.ndim - 1)
    return pl.BlockSpec((None,) + stacked.shape[1:], lambda b, j: (layer,) + zeros,
                        pipeline_mode=pl.Buffered(1))


def _mixer(x, mk, mv, layer, p):
    bsz, slen, _ = x.shape
    t = min(MIX_TILE, slen)
    tile = pl.BlockSpec((1, t, D_MODEL), lambda b, j: (b, j, 0))
    kv_spec = pl.BlockSpec((1, 1, HEADS, MEM_LEN, WIN), lambda b, j: (layer, b, 0, 0, 0))
    weights = (p["gain"], p["w1"], p["cw"], p["cb"], p["wg"], p["bg"], p["lam"], p["wdec"], p["bdec"],
               p["gn"])
    tail = (p["wb"], p["wo"])
    return pl.pallas_call(
        _mixer_kernel,
        grid=(bsz, slen // t),
        in_specs=[tile] + [_resident(w, layer) for w in weights] + [kv_spec, kv_spec]
        + [_resident(w, layer) for w in tail],
        out_specs=tile,
        out_shape=jax.ShapeDtypeStruct(x.shape, F32),
        scratch_shapes=[
            pltpu.VMEM((SUBLANES, MIX), F32),
            pltpu.VMEM((SUBLANES, MIX), F32),
            pltpu.VMEM((HEADS, DK_PAD, WIN), F32),
        ],
        compiler_params=pltpu.CompilerParams(
            dimension_semantics=("arbitrary", "arbitrary"), vmem_limit_bytes=VMEM_LIMIT),
        name="mixer",
    )(x, *weights, mk, mv, *tail)


GELU_C0 = math.sqrt(2.0 / math.pi)
GELU_C1 = GELU_C0 * 0.044715


def _ffn_kernel(x_ref, gain_ref, wup_ref, cw_ref, cb_ref, wdn_ref, fin_ref, o_ref, ut_s, *, final):
    t = x_ref.shape[1]
    j = pl.program_id(1)

    @pl.when(j == 0)
    def _():
        ut_s[...] = jnp.zeros_like(ut_s)

    x = x_ref[0]
    hb = _rms(x, gain_ref[...]).astype(BF16)

    def conv_cols(col, scale):
        cols = slice(col, col + FFN_COLS)
        up = _dot(hb, wup_ref[:, cols])
        tail = ut_s[:, cols]
        cw = cw_ref[:, cols] * scale
        y = cb_ref[:, cols] * scale + cw[FFN_CONV - 1:FFN_CONV, :] * up
        for k in range(1, FFN_CONV):
            y = y + cw[FFN_CONV - 1 - k:FFN_CONV - k, :] * _shift_rows(up, tail, k)
        ut_s[:, cols] = up[t - SUBLANES:, :]
        return y

    def hidden(c):
        gate_h = conv_cols(c * FFN_COLS, 1.0)
        half_val = conv_cols(D_FF + c * FFN_COLS, 0.5)
        th = jnp.tanh(gate_h * (GELU_C0 + GELU_C1 * (gate_h * gate_h)))
        return ((gate_h + gate_h * th) * half_val).astype(BF16)

    n_groups = D_FF // FFN_COLS
    acc = x
    act = hidden(0)
    for c in range(n_groups):
        nxt = hidden(c + 1) if c + 1 < n_groups else None
        acc = acc + _dot(act, wdn_ref[c * FFN_COLS:(c + 1) * FFN_COLS, :])
        act = nxt
    o_ref[0] = _rms(acc, fin_ref[...]) if final else acc


def _ffn(x, layer, p, final_gain, final):
    bsz, slen, _ = x.shape
    t = min(FFN_TILE, slen)
    tile = pl.BlockSpec((1, t, D_MODEL), lambda b, j: (b, j, 0))
    weights = (p["gain2"], p["wup"], p["cw2"], p["cb2"], p["wdn"])
    return pl.pallas_call(
        functools.partial(_ffn_kernel, final=final),
        grid=(bsz, slen // t),
        in_specs=[tile] + [_resident(w, layer) for w in weights] + [_resident(final_gain, 0)],
        out_specs=tile,
        out_shape=jax.ShapeDtypeStruct(x.shape, F32),
        scratch_shapes=[pltpu.VMEM((SUBLANES, 2 * D_FF), F32)],
        compiler_params=pltpu.CompilerParams(
            dimension_semantics=("arbitrary", "arbitrary"), vmem_limit_bytes=VMEM_LIMIT),
        name="ffn",
    )(x, *weights, final_gain)


def _pad_heads(w, width, padded):
    lead = w.shape[:-1]
    w = w.reshape(*lead, HEADS, width)
    w = jnp.pad(w, [(0, 0)] * len(lead) + [(0, 0), (0, padded - width)])
    return w.reshape(*lead, HEADS * padded)


def _block_diag(w):
    n, rows, cols = w.shape
    eye = jnp.eye(n, dtype=w.dtype)
    return jnp.einsum("hij,hg->higj", w, eye).reshape(n * rows, n * cols)


def _gate_weights(w_a, w_x):
    per = RNN_BLOCKS // RNN_GROUPS
    return jnp.stack([
        jnp.concatenate([_block_diag(w_a[g * per:(g + 1) * per]),
                         _block_diag(w_x[g * per:(g + 1) * per])], axis=1)
        for g in range(RNN_GROUPS)])


def _pack_weights(w_in, norm_mix, rnn_conv_w, rnn_conv_b, rg_w_a, rg_b_a, rg_w_x, rg_b_x, rg_lambda,
                  gla_w_decay, gla_b_decay, gla_norm, w_branch, w_out, norm_ffn, w_up, ffn_conv_w,
                  ffn_conv_b, w_down):
    depth = w_in.shape[0]
    bounds = [0]
    for width in IN_SPLITS:
        bounds.append(bounds[-1] + width)
    w_r, w_q, w_k, w_v, w_g, w_d, w_mq, w_gate = (
        w_in[:, :, bounds[i]:bounds[i + 1]] for i in range(len(IN_SPLITS)))
    w_q = _pad_heads(w_q, GLA_DK, DK_PAD)
    w_q = w_q.at[:, :, GLA_DK:GLA_DK + GLA_LOW_RANK].set(w_d)
    w1 = jnp.concatenate(
        [w_r, w_q, _pad_heads(w_k, GLA_DK, DK_PAD), w_v, w_g, w_mq, w_gate], axis=2).astype(BF16)
    wdec = jnp.zeros((depth, DK_PAD, HEADS * DK_PAD), F32).at[:, GLA_DK:GLA_DK + GLA_LOW_RANK, :].set(
        _pad_heads(gla_w_decay, GLA_DK, DK_PAD))
    b_a = rg_b_a.reshape(depth, RNN_GROUPS, 1, RNN_GROUP)
    b_x = rg_b_x.reshape(depth, RNN_GROUPS, 1, RNN_GROUP)
    return {
        "gain": norm_mix.reshape(depth, 1, D_MODEL),
        "w1": w1,
        "cw": rnn_conv_w,
        "cb": rnn_conv_b.reshape(depth, 1, MIX),
        "wg": jax.vmap(_gate_weights)(rg_w_a, rg_w_x).astype(BF16),
        "bg": jnp.concatenate([b_a, b_x], axis=-1),
        "lam": rg_lambda.reshape(depth, 1, MIX),
        "wdec": wdec.astype(BF16),
        "bdec": _pad_heads(gla_b_decay.reshape(depth, 1, HEADS * GLA_DK), GLA_DK, DK_PAD),
        "gn": gla_norm.reshape(depth, 1, MIX),
        "wb": w_branch.astype(BF16),
        "wo": w_out.astype(BF16),
        "gain2": norm_ffn.reshape(depth, 1, D_MODEL),
        "wup": w_up.astype(BF16),
        "cw2": ffn_conv_w,
        "cb2": ffn_conv_b.reshape(depth, 1, 2 * D_FF),
        "wdn": w_down.astype(BF16),
    }


def kernel(x, mem, norm_mix, w_in, rnn_conv_w, rnn_conv_b, rg_w_a, rg_b_a, rg_w_x, rg_b_x, rg_lambda,
           gla_w_decay, gla_b_decay, gla_norm, mem_norm, w_mem_kv, w_branch, w_out, norm_ffn, w_up,
           ffn_conv_w, ffn_conv_b, w_down, final_norm):
    depth = w_in.shape[0]
    assert x.shape[1] % CHUNK == 0 and x.shape[2] == D_MODEL and mem.shape[1:] == (MEM_LEN, D_MODEL)
    mk, mv = _memkv(mem, mem_norm, w_mem_kv)
    p = _pack_weights(w_in, norm_mix, rnn_conv_w, rnn_conv_b, rg_w_a, rg_b_a, rg_w_x, rg_b_x, rg_lambda,
                      gla_w_decay, gla_b_decay, gla_norm, w_branch, w_out, norm_ffn, w_up, ffn_conv_w,
                      ffn_conv_b, w_down)
    final_gain = final_norm.reshape(1, 1, D_MODEL)
    for l in range(depth):
        x = _mixer(x, mk, mv, l, p)
        x = _ffn(x, l, p, final_gain, final=(l == depth - 1))
    return x
```
